```python
import math
import jax, jax.numpy as jnp
from jax import lax
import numpy as np

D_MODEL = 4096
BATCH = 4
SEQ = 4096
DEPTH = 1

HEAD_DIM = 128
ATTN_GROUPS = ((128, 1), (512, 4), (2048, 16))
N_ATTN_GROUPS = len(ATTN_GROUPS)
HEADS_PER_GROUP = D_MODEL // 512
ATTN_WIDTH = HEADS_PER_GROUP * HEAD_DIM
ATTN_QKV_WIDTH = N_ATTN_GROUPS * ATTN_WIDTH
BAND_BLOCK = 128
SGU_WIDTH = D_MODEL // 2
SGU_GROUP_CH = 128
SGU_GROUPS = SGU_WIDTH // SGU_GROUP_CH
CHUNK = 128
IN_WIDTH = 3 * ATTN_QKV_WIDTH + 2 * SGU_WIDTH
ROPE_THETA = 500000.0
ROT_DIM = HEAD_DIM // 4
XA_HEADS = 4
XA_WIDTH = XA_HEADS * HEAD_DIM
N_MEM = 256
D_FF = 4 * D_MODEL
EPS = 1e-6
NEG = -1e30

kernel_name = "hybrid_dilated_attn_gmlp_gated_block"


def rms_norm(x, g):
    x32 = x.astype(jnp.float32)
    y = x32 * lax.rsqrt(jnp.mean(x32 * x32, axis=-1, keepdims=True) + EPS)
    return (y * g.astype(jnp.float32)).astype(x.dtype)


def layer_norm(x, g, b):
    x32 = x.astype(jnp.float32)
    mu = jnp.mean(x32, axis=-1, keepdims=True)
    var = jnp.mean(jnp.square(x32 - mu), axis=-1, keepdims=True)
    y = (x32 - mu) * lax.rsqrt(var + EPS)
    return (y * g.astype(jnp.float32) + b.astype(jnp.float32)).astype(x.dtype)


def rope_tables(positions):
    inv = ROPE_THETA ** (-jnp.arange(0, ROT_DIM, 2, dtype=jnp.float32) / ROT_DIM)
    ang = positions.astype(jnp.float32)[..., None] * inv
    return jnp.cos(ang), jnp.sin(ang)


def apply_partial_rope(t, cos, sin):
    c = cos[:, :, None, None, :].astype(t.dtype)
    s = sin[:, :, None, None, :].astype(t.dtype)
    half = ROT_DIM // 2
    t1, t2, rest = t[..., :half], t[..., half:ROT_DIM], t[..., ROT_DIM:]
    return jnp.concatenate([t1 * c - t2 * s, t2 * c + t1 * s, rest], axis=-1)


def banded_causal_attention(q, k, v, n_back):
    N, L, H, hd = q.shape
    nb = -(-L // BAND_BLOCK)
    Lp = nb * BAND_BLOCK
    pad = Lp - L
    qb = jnp.pad(q, ((0, 0), (0, pad), (0, 0), (0, 0))).reshape(N, nb, BAND_BLOCK, H, hd)
    kp = jnp.pad(k, ((0, 0), (BAND_BLOCK, pad), (0, 0), (0, 0))).reshape(N, nb + 1, BAND_BLOCK, H, hd)
    vp = jnp.pad(v, ((0, 0), (BAND_BLOCK, pad), (0, 0), (0, 0))).reshape(N, nb + 1, BAND_BLOCK, H, hd)
    kw = jnp.concatenate([kp[:, :-1], kp[:, 1:]], axis=2)
    vw = jnp.concatenate([vp[:, :-1], vp[:, 1:]], axis=2)
    qi = jnp.arange(BAND_BLOCK)[:, None]
    kj = jnp.arange(2 * BAND_BLOCK)[None, :]
    dist = qi + BAND_BLOCK - kj
    keypos = jnp.arange(nb)[:, None, None] * BAND_BLOCK - BAND_BLOCK + kj[None]
    mask = (dist >= 0)[None] & (dist <= n_back)[None] & (keypos >= 0)
    s = jnp.einsum('nbqhd,nbkhd->nbhqk', qb, kw).astype(jnp.float32)
    s = jnp.where(mask[None, :, None], s, NEG)
    lse = jax.nn.logsumexp(s, axis=-1)
    p = jnp.exp(s - lse[..., None])
    o = jnp.einsum('nbhqk,nbkhd->nbqhd', p.astype(v.dtype), vw)
    o = o.reshape(N, Lp, H, hd)[:, :L]
    lse = lse.transpose(0, 1, 3, 2).reshape(N, Lp, H)[:, :L]
    return o, lse


def dilated_causal_attention(q, k, v, window, dilation):
    B, S, H, hd = q.shape
    L = S // dilation

    def to_res(t):
        return t.reshape(B, L, dilation, H, hd).transpose(0, 2, 1, 3, 4).reshape(B * dilation, L, H, hd)

    o, lse = banded_causal_attention(to_res(q), to_res(k), to_res(v), window // dilation)
    o = o.reshape(B, dilation, L, H, hd).transpose(0, 2, 1, 3, 4).reshape(B, S, H, hd)
    lse = lse.reshape(B, dilation, L, H).transpose(0, 2, 1, 3).reshape(B, S, H)
    return o, lse


def setup_inputs(seed: int = 0) -> dict:
    key = jax.random.key(seed)
    ks = jax.random.split(key, 32)
    f32 = jnp.float32

    def nrm(k, shape, fan_in):
        return jax.random.normal(k, shape, f32) * (fan_in ** -0.5)

    def gain(k, shape):
        return 1.0 + 0.05 * jax.random.normal(k, shape, f32)

    x = jax.random.normal(ks[0], (BATCH, SEQ, D_MODEL), f32)
    mem = jax.random.normal(ks[1], (BATCH, N_MEM, D_MODEL), f32)
    offset = jax.random.randint(ks[2], (BATCH, 1), 0, 1024, dtype=jnp.int32)
    positions = offset + jnp.arange(SEQ, dtype=jnp.int32)[None, :]
    return {
        "x": x,
        "mem": mem,
        "positions": positions,
        "mix_pre_g": gain(ks[3], (DEPTH, D_MODEL)),
        "w_in": nrm(ks[4], (DEPTH, D_MODEL, IN_WIDTH), D_MODEL),
        "sgu_ln_g": gain(ks[5], (DEPTH, SGU_WIDTH)),
        "sgu_ln_b": 0.01 * jax.random.normal(ks[6], (DEPTH, SGU_WIDTH), f32),
        "w_spatial": nrm(ks[7], (DEPTH, SGU_GROUPS, CHUNK, CHUNK), CHUNK),
        "b_spatial": 1.0 + 0.01 * jax.random.normal(ks[8], (DEPTH, SGU_GROUPS, CHUNK), f32),
        "w_branch_a": nrm(ks[9], (DEPTH, ATTN_WIDTH, D_MODEL), ATTN_WIDTH),
        "w_branch_b": nrm(ks[10], (DEPTH, SGU_WIDTH, D_MODEL), SGU_WIDTH),
        "w_gate": nrm(ks[11], (DEPTH, D_MODEL, 2 * D_MODEL), D_MODEL),
        "b_gate": 0.01 * jax.random.normal(ks[12], (DEPTH, 2 * D_MODEL), f32),
        "w_out": nrm(ks[13], (DEPTH, D_MODEL, D_MODEL), D_MODEL),
        "mix_post_g": gain(ks[14], (DEPTH, D_MODEL)),
        "xa_pre_g": gain(ks[15], (DEPTH, D_MODEL)),
        "mem_norm_g": gain(ks[16], (DEPTH, D_MODEL)),
        "w_xq": nrm(ks[17], (DEPTH, D_MODEL, XA_WIDTH), D_MODEL),
        "w_xk": nrm(ks[18], (DEPTH, D_MODEL, XA_WIDTH), D_MODEL),
        "w_xv": nrm(ks[19], (DEPTH, D_MODEL, XA_WIDTH), D_MODEL),
        "w_xo": nrm(ks[20], (DEPTH, XA_WIDTH, D_MODEL), XA_WIDTH),
        "xa_post_g": gain(ks[21], (DEPTH, D_MODEL)),
        "mlp_pre_g": gain(ks[22], (DEPTH, D_MODEL)),
        "w_up": nrm(ks[23], (DEPTH, D_MODEL, D_FF), D_MODEL),
        "w_down": nrm(ks[24], (DEPTH, D_FF, D_MODEL), D_FF),
        "mlp_post_g": gain(ks[25], (DEPTH, D_MODEL)),
    }


def reference(x, mem, positions, mix_pre_g, w_in, sgu_ln_g, sgu_ln_b, w_spatial, b_spatial,
              w_branch_a, w_branch_b, w_gate, b_gate, w_out, mix_post_g, xa_pre_g, mem_norm_g,
              w_xq, w_xk, w_xv, w_xo, xa_post_g, mlp_pre_g, w_up, w_down, mlp_post_g):
    B, S, _ = x.shape
    M = mem.shape[1]
    dt = x.dtype
    scale = HEAD_DIM ** -0.5
    cos, sin = rope_tables(positions)
    causal_tri = jnp.tril(jnp.ones((CHUNK, CHUNK), dtype=dt))

    for l in range(DEPTH):
        h = rms_norm(x, mix_pre_g[l])
        proj = h @ w_in[l]
        q_all, k_all, v_all, u_b, v_b = jnp.split(
            proj, [ATTN_QKV_WIDTH, 2 * ATTN_QKV_WIDTH, 3 * ATTN_QKV_WIDTH,
                   3 * ATTN_QKV_WIDTH + SGU_WIDTH], axis=-1)
        gshape = (B, S, N_ATTN_GROUPS, HEADS_PER_GROUP, HEAD_DIM)
        q_all = apply_partial_rope(q_all.reshape(gshape), cos, sin) * jnp.asarray(scale, dt)
        k_all = apply_partial_rope(k_all.reshape(gshape), cos, sin)
        v_all = v_all.reshape(gshape)

        outs, lses = [], []
        for g, (window, dilation) in enumerate(ATTN_GROUPS):
            o, lse = dilated_causal_attention(q_all[:, :, g], k_all[:, :, g], v_all[:, :, g],
                                              window, dilation)
            outs.append(o)
            lses.append(lse)
        alpha = jax.nn.softmax(jnp.stack(lses, axis=0), axis=0)
        y_a = jnp.sum(alpha[..., None] * jnp.stack(outs, axis=0).astype(jnp.float32), axis=0)
        y_a = y_a.astype(dt).reshape(B, S, ATTN_WIDTH)

        u_b = jax.nn.gelu(u_b)
        v_b = layer_norm(jax.nn.gelu(v_b), sgu_ln_g[l], sgu_ln_b[l])
        vc = v_b.reshape(B, S // CHUNK, CHUNK, SGU_GROUPS, SGU_GROUP_CH)
        ws = w_spatial[l] * causal_tri
        mixed = jnp.einsum('gij,bnjgc->bnigc', ws, vc) + b_spatial[l].T[None, None, :, :, None]
        y_b = u_b * mixed.reshape(B, S, SGU_WIDTH)

        gates = jax.nn.sigmoid(h @ w_gate[l] + b_gate[l])
        g_a, g_b = jnp.split(gates, 2, axis=-1)
        merged = g_a * (y_a @ w_branch_a[l]) + g_b * (y_b @ w_branch_b[l])
        x = x + rms_norm(merged @ w_out[l], mix_post_g[l])

        h = rms_norm(x, xa_pre_g[l])
        m = rms_norm(mem, mem_norm_g[l])
        q = (h @ w_xq[l]).reshape(B, S, XA_HEADS, HEAD_DIM) * jnp.asarray(scale, dt)
        k = (m @ w_xk[l]).reshape(B, M, XA_HEADS, HEAD_DIM)
        v = (m @ w_xv[l]).reshape(B, M, XA_HEADS, HEAD_DIM)
        p = jax.nn.softmax(jnp.einsum('bshd,bmhd->bhsm', q, k).astype(jnp.float32), axis=-1)
        o = jnp.einsum('bhsm,bmhd->bshd', p.astype(dt), v).reshape(B, S, XA_WIDTH)
        x = x + rms_norm(o @ w_xo[l], xa_post_g[l])

        h = rms_norm(x, mlp_pre_g[l])
        a = jnp.square(jax.nn.relu(h @ w_up[l]))
        x = x + rms_norm(a @ w_down[l], mlp_post_g[l])
    return x
```

```python
import functools
import math

import jax
import jax.numpy as jnp
from jax import lax
from jax.experimental import pallas as pl
from jax.experimental.pallas import tpu as pltpu

F32 = jnp.float32
BF16 = jnp.bfloat16

D_MODEL = 4096
HEAD_DIM = 128
ATTN_GROUPS = ((128, 1), (512, 4), (2048, 16))
N_GROUPS = len(ATTN_GROUPS)
HEADS_PER_GROUP = 8
ATTN_WIDTH = HEADS_PER_GROUP * HEAD_DIM
ATTN_QKV_WIDTH = N_GROUPS * ATTN_WIDTH
BAND = 128
SGU_WIDTH = D_MODEL // 2
SGU_GROUP_CH = 128
SGU_GROUPS = SGU_WIDTH // SGU_GROUP_CH
CHUNK = 128
IN_WIDTH = 3 * ATTN_QKV_WIDTH + 2 * SGU_WIDTH
ROPE_THETA = 500000.0
ROT_DIM = HEAD_DIM // 4
ROT_HALF = ROT_DIM // 2
XA_HEADS = 4
XA_WIDTH = XA_HEADS * HEAD_DIM
D_FF = 4 * D_MODEL
EPS = 1e-6
NEG = -1e30
ATTN_SCALE = HEAD_DIM ** -0.5
LANES = 128

VMEM_LIMIT = 56 * 1024 * 1024


def _params(sem, vmem=VMEM_LIMIT):
    return pltpu.CompilerParams(dimension_semantics=sem, vmem_limit_bytes=vmem)


def _gelu_tanh(x):
    c = math.sqrt(2.0 / math.pi)
    return 0.5 * x * (1.0 + jnp.tanh(c * (x + 0.044715 * (x * x * x))))


def _rope_table_kernel(pos_ref, inv_ref, c_ref, sa_ref, sb_ref):
    ang = pos_ref[...] * inv_ref[...]
    c = jnp.cos(ang)
    s = jnp.sin(ang)
    lane = lax.broadcasted_iota(jnp.int32, ang.shape, 1)
    c_ref[...] = jnp.where(lane < ROT_DIM, c, 1.0)
    sa_ref[...] = jnp.where(lane < ROT_HALF, -s, 0.0)
    sb_ref[...] = jnp.where((lane >= ROT_HALF) & (lane < ROT_DIM), s, 0.0)


def _rope_tables(positions):
    m = positions.size
    tm = 2048
    inv = ROPE_THETA ** (-jnp.arange(0, ROT_DIM, 2, dtype=F32) / ROT_DIM)
    inv_lane = jnp.concatenate([inv, inv, jnp.zeros((LANES - ROT_DIM,), F32)])[None, :]
    pos = positions.astype(F32).reshape(m, 1)
    out = jax.ShapeDtypeStruct((m, LANES), F32)
    spec = pl.BlockSpec((tm, LANES), lambda i: (i, 0))
    return pl.pallas_call(
        _rope_table_kernel,
        grid=(m // tm,),
        in_specs=[pl.BlockSpec((tm, 1), lambda i: (i, 0)), pl.BlockSpec((1, LANES), lambda i: (0, 0))],
        out_specs=[spec, spec, spec],
        out_shape=[out, out, out],
        compiler_params=_params(("parallel",)),
        name="rope_tables",
    )(pos, inv_lane)


def _rmsnorm_kernel(x_ref, g_ref, o_ref):
    x = x_ref[...]
    y = x * lax.rsqrt(jnp.mean(x * x, axis=-1, keepdims=True) + EPS)
    o_ref[...] = (y * g_ref[...]).astype(o_ref.dtype)


def _rmsnorm(x2d, g, out_dtype=BF16, tm=512):
    m, d = x2d.shape
    return pl.pallas_call(
        _rmsnorm_kernel,
        grid=(m // tm,),
        in_specs=[pl.BlockSpec((tm, d), lambda i: (i, 0)), pl.BlockSpec((1, d), lambda i: (0, 0))],
        out_specs=pl.BlockSpec((tm, d), lambda i: (i, 0)),
        out_shape=jax.ShapeDtypeStruct((m, d), out_dtype),
        compiler_params=_params(("parallel",)),
        name="rmsnorm",
    )(x2d, g.reshape(1, d))


_QK_BLOCKS = 2 * ATTN_QKV_WIDTH // 1024
_QKV_BLOCKS = 3 * ATTN_QKV_WIDTH // 1024


def _inproj_kernel(h_ref, w_ref, c_ref, sa_ref, sb_ref, o_ref):
    j = pl.program_id(1)
    acc = jnp.dot(h_ref[...], w_ref[...], preferred_element_type=F32)
    tn = acc.shape[1]

    @pl.when(j < _QK_BLOCKS)
    def _():
        c = c_ref[...]
        sa = sa_ref[...]
        sb = sb_ref[...]
        scale = jnp.where(j < _QK_BLOCKS // 2, ATTN_SCALE, 1.0).astype(F32)
        for hh in range(tn // HEAD_DIM):
            xh = acc[:, hh * HEAD_DIM:(hh + 1) * HEAD_DIM]
            y = xh * c + pltpu.roll(xh, LANES - ROT_HALF, 1) * sa + pltpu.roll(xh, ROT_HALF, 1) * sb
            o_ref[:, hh * HEAD_DIM:(hh + 1) * HEAD_DIM] = (y * scale).astype(o_ref.dtype)

    @pl.when((j >= _QK_BLOCKS) & (j < _QKV_BLOCKS))
    def _():
        o_ref[...] = acc.astype(o_ref.dtype)

    @pl.when(j >= _QKV_BLOCKS)
    def _():
        o_ref[...] = _gelu_tanh(acc).astype(o_ref.dtype)


def _inproj(h, w_in_bf16, c_tab, sa_tab, sb_tab, tm=1024, tn=1024):
    m, k = h.shape
    n = w_in_bf16.shape[1]
    tab_spec = pl.BlockSpec((tm, LANES), lambda i, j: (i, 0))
    return pl.pallas_call(
        _inproj_kernel,
        grid=(m // tm, n // tn),
        in_specs=[pl.BlockSpec((tm, k), lambda i, j: (i, 0)),
                  pl.BlockSpec((k, tn), lambda i, j: (0, j)),
                  tab_spec, tab_spec, tab_spec],
        out_specs=pl.BlockSpec((tm, tn), lambda i, j: (i, j)),
        out_shape=jax.ShapeDtypeStruct((m, n), BF16),
        compiler_params=_params(("parallel", "arbitrary")),
        name="inproj",
    )(h, w_in_bf16, c_tab, sa_tab, sb_tab)


def _band_attn_kernel(q_ref, kp_ref, kc_ref, vp_ref, vc_ref, o_ref, lse_ref):
    i = pl.program_id(2)
    row = lax.broadcasted_iota(jnp.int32, (BAND, BAND), 0)
    col = lax.broadcasted_iota(jnp.int32, (BAND, BAND), 1)
    mask_cur = col <= row
    mask_prev = (col >= row) & (i > 0)
    lse_tile = jnp.zeros((BAND, LANES), F32)
    dn = (((1,), (1,)), ((), ()))
    for hh in range(HEADS_PER_GROUP):
        sl = slice(hh * HEAD_DIM, (hh + 1) * HEAD_DIM)
        q = q_ref[:, sl]
        s_c = lax.dot_general(q, kc_ref[:, sl], dn, preferred_element_type=F32)
        s_p = lax.dot_general(q, kp_ref[:, sl], dn, preferred_element_type=F32)
        s_c = jnp.where(mask_cur, s_c, NEG)
        s_p = jnp.where(mask_prev, s_p, NEG)
        mx = jnp.maximum(jnp.max(s_c, axis=-1, keepdims=True), jnp.max(s_p, axis=-1, keepdims=True))
        p_c = jnp.exp(s_c - mx)
        p_p = jnp.exp(s_p - mx)
        den = jnp.sum(p_c, axis=-1, keepdims=True) + jnp.sum(p_p, axis=-1, keepdims=True)
        o = jnp.dot(p_c.astype(BF16), vc_ref[:, sl], preferred_element_type=F32)
        o = o + jnp.dot(p_p.astype(BF16), vp_ref[:, sl], preferred_element_type=F32)
        o_ref[:, sl] = (o / den).astype(o_ref.dtype)
        lse_tile = jnp.where(col == hh, mx + jnp.log(den), lse_tile)
    lse_ref[...] = lse_tile


def _band_attention(proj, g, batch, seq):
    _, dil = ATTN_GROUPS[g]
    sub = seq // dil
    width = IN_WIDTH // ATTN_WIDTH
    proj3 = proj.reshape(batch, sub, dil * IN_WIDTH)
    qcol, kcol, vcol = g, N_GROUPS + g, 2 * N_GROUPS + g

    def spec(col, prev):
        if prev:
            return pl.BlockSpec((None, BAND, ATTN_WIDTH),
                                lambda b, r, i: (b, jnp.maximum(i - 1, 0), r * width + col))
        return pl.BlockSpec((None, BAND, ATTN_WIDTH), lambda b, r, i: (b, i, r * width + col))

    o, lse = pl.pallas_call(
        _band_attn_kernel,
        grid=(batch, dil, sub // BAND),
        in_specs=[spec(qcol, False), spec(kcol, True), spec(kcol, False), spec(vcol, True), spec(vcol, False)],
        out_specs=[pl.BlockSpec((None, BAND, ATTN_WIDTH), lambda b, r, i: (b, i, r)),
                   pl.BlockSpec((None, BAND, LANES), lambda b, r, i: (b, i, r))],
        out_shape=[jax.ShapeDtypeStruct((batch, sub, dil * ATTN_WIDTH), BF16),
                   jax.ShapeDtypeStruct((batch, sub, dil * LANES), F32)],
        compiler_params=_params(("parallel", "parallel", "arbitrary")),
        name=f"band_attn_g{g}",
    )(proj3, proj3, proj3, proj3, proj3)
    return o.reshape(batch * seq, ATTN_WIDTH), lse.reshape(batch * seq, LANES)


def _combine_kernel(o0_ref, o1_ref, o2_ref, l0_ref, l1_ref, l2_ref, y_ref):
    l0, l1, l2 = l0_ref[...], l1_ref[...], l2_ref[...]
    mx = jnp.maximum(jnp.maximum(l0, l1), l2)
    e0, e1, e2 = jnp.exp(l0 - mx), jnp.exp(l1 - mx), jnp.exp(l2 - mx)
    den = e0 + e1 + e2
    a0, a1, a2 = e0 / den, e1 / den, e2 / den
    rows = l0.shape[0]
    for hh in range(HEADS_PER_GROUP):
        sl = slice(hh * HEAD_DIM, (hh + 1) * HEAD_DIM)
        b0 = jnp.broadcast_to(a0[:, hh:hh + 1], (rows, HEAD_DIM))
        b1 = jnp.broadcast_to(a1[:, hh:hh + 1], (rows, HEAD_DIM))
        b2 = jnp.broadcast_to(a2[:, hh:hh + 1], (rows, HEAD_DIM))
        y = b0 * o0_ref[:, sl].astype(F32) + b1 * o1_ref[:, sl].astype(F32) + b2 * o2_ref[:, sl].astype(F32)
        y_ref[:, sl] = y.astype(y_ref.dtype)


def _combine(os, lses, tm=512):
    m = os[0].shape[0]
    ospec = pl.BlockSpec((tm, ATTN_WIDTH), lambda i: (i, 0))
    lspec = pl.BlockSpec((tm, LANES), lambda i: (i, 0))
    return pl.pallas_call(
        _combine_kernel,
        grid=(m // tm,),
        in_specs=[ospec, ospec, ospec, lspec, lspec, lspec],
        out_specs=ospec,
        out_shape=jax.ShapeDtypeStruct((m, ATTN_WIDTH), BF16),
        compiler_params=_params(("parallel",)),
        name="attn_combine",
    )(*os, *lses)


def _sgu_kernel(u0_ref, u1_ref, v0_ref, v1_ref, g_ref, b_ref, ws_ref, bt_ref, o_ref):
    half = SGU_WIDTH // 2
    v0 = v0_ref[...].astype(F32)
    v1 = v1_ref[...].astype(F32)
    mu = (jnp.sum(v0, axis=-1, keepdims=True) + jnp.sum(v1, axis=-1, keepdims=True)) / SGU_WIDTH
    d0, d1 = v0 - mu, v1 - mu
    var = (jnp.sum(d0 * d0, axis=-1, keepdims=True) + jnp.sum(d1 * d1, axis=-1, keepdims=True)) / SGU_WIDTH
    inv = lax.rsqrt(var + EPS)
    vn = ((d0 * inv * g_ref[:, :half] + b_ref[:, :half]).astype(BF16),
          (d1 * inv * g_ref[:, half:] + b_ref[:, half:]).astype(BF16))
    u_refs = (u0_ref, u1_ref)
    row = lax.broadcasted_iota(jnp.int32, (CHUNK, CHUNK), 0)
    col = lax.broadcasted_iota(jnp.int32, (CHUNK, CHUNK), 1)
    tri = col <= row
    per_half = SGU_GROUPS // 2
    for gg in range(SGU_GROUPS):
        hf, gl = divmod(gg, per_half)
        sl = slice(gl * SGU_GROUP_CH, (gl + 1) * SGU_GROUP_CH)
        w = jnp.where(tri, ws_ref[gg], 0.0).astype(BF16)
        mixed = jnp.dot(w, vn[hf][:, sl], preferred_element_type=F32) + bt_ref[:, gg:gg + 1]
        osl = slice(gg * SGU_GROUP_CH, (gg + 1) * SGU_GROUP_CH)
        o_ref[:, osl] = (u_refs[hf][:, sl].astype(F32) * mixed).astype(o_ref.dtype)


def _sgu(proj, ln_g, ln_b, w_spatial, b_spatial):
    m = proj.shape[0]
    half = SGU_WIDTH // 2
    ucol = 3 * ATTN_QKV_WIDTH // half
    vec = pl.BlockSpec((1, SGU_WIDTH), lambda i: (0, 0))

    def col_spec(c):
        return pl.BlockSpec((CHUNK, half), lambda i: (i, c))

    return pl.pallas_call(
        _sgu_kernel,
        grid=(m // CHUNK,),
        in_specs=[col_spec(ucol), col_spec(ucol + 1), col_spec(ucol + 2), col_spec(ucol + 3),
                  vec, vec,
                  pl.BlockSpec((SGU_GROUPS, CHUNK, CHUNK), lambda i: (0, 0, 0)),
                  pl.BlockSpec((CHUNK, SGU_GROUPS), lambda i: (0, 0))],
        out_specs=pl.BlockSpec((CHUNK, SGU_WIDTH), lambda i: (i, 0)),
        out_shape=jax.ShapeDtypeStruct((m, SGU_WIDTH), BF16),
        compiler_params=_params(("parallel",)),
        name="sgu",
    )(proj, proj, proj, proj, ln_g.reshape(1, -1), ln_b.reshape(1, -1), w_spatial, b_spatial.T)


def _merge_kernel(h_ref, ya_ref, yb_ref, wga_ref, wgb_ref, wa_ref, wb_ref, bga_ref, bgb_ref, o_ref):
    h = h_ref[...]
    ga = jax.nn.sigmoid(jnp.dot(h, wga_ref[...], preferred_element_type=F32) + bga_ref[...])
    a = jnp.dot(ya_ref[...], wa_ref[...], preferred_element_type=F32)
    acc = ga * a
    gb = jax.nn.sigmoid(jnp.dot(h, wgb_ref[...], preferred_element_type=F32) + bgb_ref[...])
    b = jnp.dot(yb_ref[...], wb_ref[...], preferred_element_type=F32)
    o_ref[...] = (acc + gb * b).astype(o_ref.dtype)


def _merge(h, ya, yb, w_gate, b_gate, w_a, w_b, tm=512, tn=512):
    m = h.shape[0]
    nb = D_MODEL // tn
    b_gate = b_gate.reshape(1, -1)
    return pl.pallas_call(
        _merge_kernel,
        grid=(m // tm, nb),
        in_specs=[pl.BlockSpec((tm, D_MODEL), lambda i, j: (i, 0)),
                  pl.BlockSpec((tm, ATTN_WIDTH), lambda i, j: (i, 0)),
                  pl.BlockSpec((tm, SGU_WIDTH), lambda i, j: (i, 0)),
                  pl.BlockSpec((D_MODEL, tn), lambda i, j: (0, j)),
                  pl.BlockSpec((D_MODEL, tn), lambda i, j: (0, j + nb)),
                  pl.BlockSpec((ATTN_WIDTH, tn), lambda i, j: (0, j)),
                  pl.BlockSpec((SGU_WIDTH, tn), lambda i, j: (0, j)),
                  pl.BlockSpec((1, tn), lambda i, j: (0, j)),
                  pl.BlockSpec((1, tn), lambda i, j: (0, j + nb))],
        out_specs=pl.BlockSpec((tm, tn), lambda i, j: (i, j)),
        out_shape=jax.ShapeDtypeStruct((m, D_MODEL), BF16),
        compiler_params=_params(("parallel", "arbitrary")),
        name="gated_merge",
    )(h, ya, yb, w_gate, w_gate, w_a, w_b, b_gate, b_gate)


def _matmul_kernel(a_ref, w_ref, o_ref, *, act):
    acc = jnp.dot(a_ref[...], w_ref[...], preferred_element_type=F32)
    if act == "relu2":
        r = jnp.maximum(acc, 0.0)
        acc = r * r
    o_ref[...] = acc.astype(o_ref.dtype)


def _matmul(a, w, out_dtype, act=None, tm=1024, tn=1024, name="matmul"):
    m, k = a.shape
    n = w.shape[1]
    tm, tn = min(tm, m), min(tn, n)
    return pl.pallas_call(
        functools.partial(_matmul_kernel, act=act),
        grid=(m // tm, n // tn),
        in_specs=[pl.BlockSpec((tm, k), lambda i, j: (i, 0)), pl.BlockSpec((k, tn), lambda i, j: (0, j))],
        out_specs=pl.BlockSpec((tm, tn), lambda i, j: (i, j)),
        out_shape=jax.ShapeDtypeStruct((m, n), out_dtype),
        compiler_params=_params(("parallel", "arbitrary")),
        name=name,
    )(a, w)


def _matmul_kacc_kernel(a_ref, w_ref, o_ref, acc_ref):
    k = pl.program_id(2)

    @pl.when(k == 0)
    def _():
        acc_ref[...] = jnp.zeros_like(acc_ref)

    acc_ref[...] += jnp.dot(a_ref[...], w_ref[...], preferred_element_type=F32)

    @pl.when(k == pl.num_programs(2) - 1)
    def _():
        o_ref[...] = acc_ref[...].astype(o_ref.dtype)


def _matmul_kacc(a, w, out_dtype, tm=1024, tn=1024, tk=2048, name="matmul_kacc"):
    m, k = a.shape
    n = w.shape[1]
    return pl.pallas_call(
        _matmul_kacc_kernel,
        grid=(m // tm, n // tn, k // tk),
        in_specs=[pl.BlockSpec((tm, tk), lambda i, j, kk: (i, kk)),
                  pl.BlockSpec((tk, tn), lambda i, j, kk: (kk, j))],
        out_specs=pl.BlockSpec((tm, tn), lambda i, j, kk: (i, j)),
        out_shape=jax.ShapeDtypeStruct((m, n), out_dtype),
        scratch_shapes=[pltpu.VMEM((tm, tn), F32)],
        compiler_params=_params(("parallel", "parallel", "arbitrary")),
        name=name,
    )(a, w)


def _norm_residual_kernel(y_ref, x_ref, g_ref, gn_ref, xo_ref, hn_ref):
    y = y_ref[...]
    x_new = x_ref[...] + y * lax.rsqrt(jnp.mean(y * y, axis=-1, keepdims=True) + EPS) * g_ref[...]
    xo_ref[...] = x_new
    hn = x_new * lax.rsqrt(jnp.mean(x_new * x_new, axis=-1, keepdims=True) + EPS) * gn_ref[...]
    hn_ref[...] = hn.astype(hn_ref.dtype)


def _norm_residual_last_kernel(y_ref, x_ref, g_ref, xo_ref):
    y = y_ref[...]
    xo_ref[...] = x_ref[...] + y * lax.rsqrt(jnp.mean(y * y, axis=-1, keepdims=True) + EPS) * g_ref[...]


def _norm_residual(y, x, g, g_next=None, tm=256):
    m, d = y.shape
    row = pl.BlockSpec((tm, d), lambda i: (i, 0))
    vec = pl.BlockSpec((1, d), lambda i: (0, 0))
    if g_next is None:
        return pl.pallas_call(
            _norm_residual_last_kernel,
            grid=(m // tm,),
            in_specs=[row, row, vec],
            out_specs=row,
            out_shape=jax.ShapeDtypeStruct((m, d), F32),
            compiler_params=_params(("parallel",)),
            name="norm_residual_last",
        )(y, x, g.reshape(1, d))
    return pl.pallas_call(
        _norm_residual_kernel,
        grid=(m // tm,),
        in_specs=[row, row, vec, vec],
        out_specs=[row, row],
        out_shape=[jax.ShapeDtypeStruct((m, d), F32), jax.ShapeDtypeStruct((m, d), BF16)],
        compiler_params=_params(("parallel",)),
        name="norm_residual",
    )(y, x, g.reshape(1, d), g_next.reshape(1, d))


def _xattn_kernel(h_ref, wq_ref, k_ref, v_ref, wo_ref, o_ref):
    q = jnp.dot(h_ref[...], wq_ref[...], preferred_element_type=F32) * ATTN_SCALE
    q = q.astype(BF16)
    dn = (((1,), (1,)), ((), ()))
    outs = []
    for hh in range(XA_HEADS):
        sl = slice(hh * HEAD_DIM, (hh + 1) * HEAD_DIM)
        s = lax.dot_general(q[:, sl], k_ref[:, sl], dn, preferred_element_type=F32)
        mx = jnp.max(s, axis=-1, keepdims=True)
        p = jnp.exp(s - mx)
        den = jnp.sum(p, axis=-1, keepdims=True)
        o = jnp.dot(p.astype(BF16), v_ref[:, sl], preferred_element_type=F32)
        outs.append((o / den).astype(BF16))
    o_all = jnp.concatenate(outs, axis=-1)
    o_ref[...] = jnp.dot(o_all, wo_ref[...], preferred_element_type=F32)


def _xattn(h, w_xq, k_mem, v_mem, w_xo, batch, seq, tm=512):
    m = h.shape[0]
    n_mem = k_mem.shape[0] // batch
    per_b = seq // tm
    return pl.pallas_call(
        _xattn_kernel,
        grid=(m // tm,),
        in_specs=[pl.BlockSpec((tm, D_MODEL), lambda i: (i, 0)),
                  pl.BlockSpec((D_MODEL, XA_WIDTH), lambda i: (0, 0)),
                  pl.BlockSpec((n_mem, XA_WIDTH), lambda i: (i // per_b, 0)),
                  pl.BlockSpec((n_mem, XA_WIDTH), lambda i: (i // per_b, 0)),
                  pl.BlockSpec((XA_WIDTH, D_MODEL), lambda i: (0, 0))],
        out_specs=pl.BlockSpec((tm, D_MODEL), lambda i: (i, 0)),
        out_shape=jax.ShapeDtypeStruct((m, D_MODEL), F32),
        compiler_params=_params(("parallel",)),
        name="cross_attn",
    )(h, w_xq, k_mem, v_mem, w_xo)


def kernel(x, mem, positions, mix_pre_g, w_in, sgu_ln_g, sgu_ln_b, w_spatial, b_spatial, w_branch_a, w_branch_b, w_gate, b_gate, w_out, mix_post_g, xa_pre_g, mem_norm_g, w_xq, w_xk, w_xv, w_xo, xa_post_g, mlp_pre_g, w_up, w_down, mlp_post_g):
    batch, seq, d = x.shape
    n_mem = mem.shape[1]
    depth = w_in.shape[0]
    m = batch * seq
    x2 = x.reshape(m, d)
    mem2 = mem.reshape(batch * n_mem, d)
    c_tab, sa_tab, sb_tab = _rope_tables(positions)

    for l in range(depth):
        bf = lambda w: w[l].astype(BF16)
        h = _rmsnorm(x2, mix_pre_g[l])
        proj = _inproj(h, bf(w_in), c_tab, sa_tab, sb_tab)
        os, lses = [], []
        for g in range(N_GROUPS):
            o, lse = _band_attention(proj, g, batch, seq)
            os.append(o)
            lses.append(lse)
        y_a = _combine(os, lses)
        y_b = _sgu(proj, sgu_ln_g[l], sgu_ln_b[l], w_spatial[l], b_spatial[l])
        merged = _merge(h, y_a, y_b, bf(w_gate), b_gate[l], bf(w_branch_a), bf(w_branch_b))
        y = _matmul(merged, bf(w_out), F32, name="out_proj")
        x2, h = _norm_residual(y, x2, mix_post_g[l], xa_pre_g[l])

        mn = _rmsnorm(mem2, mem_norm_g[l])
        k_mem = _matmul(mn, bf(w_xk), BF16, name="mem_k")
        v_mem = _matmul(mn, bf(w_xv), BF16, name="mem_v")
        y = _xattn(h, bf(w_xq), k_mem, v_mem, bf(w_xo), batch, seq)
        x2, h = _norm_residual(y, x2, xa_post_g[l], mlp_pre_g[l])

        a = _matmul(h, bf(w_up), BF16, act="relu2", name="mlp_up")
        y = _matmul_kacc(a, bf(w_down), F32, name="mlp_down")
        if l + 1 < depth:
            x2, h = _norm_residual(y, x2, mlp_post_g[l], mix_pre_g[l + 1])
        else:
            x2 = _norm_residual(y, x2, mlp_post_g[l])
    return x2.reshape(batch, seq, d)
```

```python
import functools
import math

import jax
import jax.numpy as jnp
from jax import lax
from jax.experimental import pallas as pl
from jax.experimental.pallas import tpu as pltpu

F32 = jnp.float32
BF16 = jnp.bfloat16

D_MODEL = 4096
HEAD_DIM = 128
ATTN_GROUPS = ((128, 1), (512, 4), (2048, 16))
N_GROUPS = len(ATTN_GROUPS)
HEADS_PER_GROUP = 8
ATTN_WIDTH = HEADS_PER_GROUP * HEAD_DIM
ATTN_QKV_WIDTH = N_GROUPS * ATTN_WIDTH
BAND = 128
SGU_WIDTH = D_MODEL // 2
SGU_GROUP_CH = 128
SGU_GROUPS = SGU_WIDTH // SGU_GROUP_CH
CHUNK = 128
IN_WIDTH = 3 * ATTN_QKV_WIDTH + 2 * SGU_WIDTH
ROPE_THETA = 500000.0
ROT_DIM = HEAD_DIM // 4
ROT_HALF = ROT_DIM // 2
XA_HEADS = 4
XA_WIDTH = XA_HEADS * HEAD_DIM
D_FF = 4 * D_MODEL
EPS = 1e-6
NEG = -1e30
ATTN_SCALE = HEAD_DIM ** -0.5
LANES = 128

VMEM_LIMIT = 56 * 1024 * 1024


def _params(sem, vmem=VMEM_LIMIT):
    return pltpu.CompilerParams(dimension_semantics=sem, vmem_limit_bytes=vmem)


def _gelu_tanh(x):
    c = math.sqrt(2.0 / math.pi)
    return 0.5 * x * (1.0 + jnp.tanh(c * (x + 0.044715 * (x * x * x))))


def _rope_table_kernel(pos_ref, inv_ref, c_ref, sa_ref, sb_ref):
    ang = pos_ref[...] * inv_ref[...]
    c = jnp.cos(ang)
    s = jnp.sin(ang)
    lane = lax.broadcasted_iota(jnp.int32, ang.shape, 1)
    c_ref[...] = jnp.where(lane < ROT_DIM, c, 1.0)
    sa_ref[...] = jnp.where(lane < ROT_HALF, -s, 0.0)
    sb_ref[...] = jnp.where((lane >= ROT_HALF) & (lane < ROT_DIM), s, 0.0)


def _rope_tables(positions):
    m = positions.size
    tm = 2048
    inv = ROPE_THETA ** (-jnp.arange(0, ROT_DIM, 2, dtype=F32) / ROT_DIM)
    inv_lane = jnp.concatenate([inv, inv, jnp.zeros((LANES - ROT_DIM,), F32)])[None, :]
    pos = positions.astype(F32).reshape(m, 1)
    out = jax.ShapeDtypeStruct((m, LANES), F32)
    spec = pl.BlockSpec((tm, LANES), lambda i: (i, 0))
    return pl.pallas_call(
        _rope_table_kernel,
        grid=(m // tm,),
        in_specs=[pl.BlockSpec((tm, 1), lambda i: (i, 0)), pl.BlockSpec((1, LANES), lambda i: (0, 0))],
        out_specs=[spec, spec, spec],
        out_shape=[out, out, out],
        compiler_params=_params(("parallel",)),
        name="rope_tables",
    )(pos, inv_lane)


def _rmsnorm_kernel(x_ref, g_ref, o_ref):
    x = x_ref[...]
    y = x * lax.rsqrt(jnp.mean(x * x, axis=-1, keepdims=True) + EPS)
    o_ref[...] = (y * g_ref[...]).astype(o_ref.dtype)


def _rmsnorm(x2d, g, out_dtype=BF16, tm=512):
    m, d = x2d.shape
    return pl.pallas_call(
        _rmsnorm_kernel,
        grid=(m // tm,),
        in_specs=[pl.BlockSpec((tm, d), lambda i: (i, 0)), pl.BlockSpec((1, d), lambda i: (0, 0))],
        out_specs=pl.BlockSpec((tm, d), lambda i: (i, 0)),
        out_shape=jax.ShapeDtypeStruct((m, d), out_dtype),
        compiler_params=_params(("parallel",)),
        name="rmsnorm",
    )(x2d, g.reshape(1, d))


_UV_TILES = 2 * SGU_WIDTH // ATTN_WIDTH
_DOT_CHUNK = 2 * HEAD_DIM


def _inproj_qkv_kernel(h_ref, w_ref, c_ref, sa_ref, sb_ref, o_ref, *scratch, dil):
    kind = pl.program_id(1)
    tm = h_ref.shape[0]
    c = c_ref[...]
    sa = sa_ref[...]
    sb = sb_ref[...]
    rotate = kind < 2
    scale = jnp.where(kind == 0, ATTN_SCALE, 1.0).astype(F32)
    h = h_ref[...]
    for cc in range(ATTN_WIDTH // _DOT_CHUNK):
        acc = jnp.dot(h, w_ref[:, cc * _DOT_CHUNK:(cc + 1) * _DOT_CHUNK], preferred_element_type=F32)
        for hc in range(_DOT_CHUNK // HEAD_DIM):
            hh = cc * (_DOT_CHUNK // HEAD_DIM) + hc
            xh = acc[:, hc * HEAD_DIM:(hc + 1) * HEAD_DIM]
            y = xh * c + pltpu.roll(xh, LANES - ROT_HALF, 1) * sa + pltpu.roll(xh, ROT_HALF, 1) * sb
            y = jnp.where(rotate, y * scale, xh)
            if dil == 1:
                o_ref[0, :, hh * HEAD_DIM:(hh + 1) * HEAD_DIM] = y.astype(o_ref.dtype)
            else:
                scratch[0][hh] = y
    if dil > 1:
        for hh in range(HEADS_PER_GROUP):
            for r in range(dil):
                rows = scratch[0][hh, pl.ds(r, tm // dil, stride=dil), :]
                o_ref[r, :, hh * HEAD_DIM:(hh + 1) * HEAD_DIM] = rows.astype(o_ref.dtype)


def _inproj_uv_kernel(h_ref, w_ref, o_ref):
    h = h_ref[...]
    for cc in range(ATTN_WIDTH // _DOT_CHUNK):
        sl = slice(cc * _DOT_CHUNK, (cc + 1) * _DOT_CHUNK)
        acc = jnp.dot(h, w_ref[:, sl], preferred_element_type=F32)
        o_ref[:, sl] = _gelu_tanh(acc).astype(o_ref.dtype)


def _inproj_qkv(h, w_in_bf16, c_tab, sa_tab, sb_tab, g, batch, seq, tm=1024):
    m, k = h.shape
    tn = ATTN_WIDTH
    dil = ATTN_GROUPS[g][1]
    per_b = seq // tm
    tab_spec = pl.BlockSpec((tm, LANES), lambda i, kind: (i, 0))
    return pl.pallas_call(
        functools.partial(_inproj_qkv_kernel, dil=dil),
        grid=(m // tm, 3),
        in_specs=[pl.BlockSpec((tm, k), lambda i, kind: (i, 0)),
                  pl.BlockSpec((k, tn), lambda i, kind: (0, kind * N_GROUPS + g)),
                  tab_spec, tab_spec, tab_spec],
        out_specs=pl.BlockSpec((None, dil, tm // dil, tn), lambda i, kind: (i // per_b, 0, i % per_b, kind)),
        out_shape=jax.ShapeDtypeStruct((batch, dil, seq // dil, 3 * tn), BF16),
        scratch_shapes=[pltpu.VMEM((HEADS_PER_GROUP, tm, HEAD_DIM), F32)] if dil > 1 else [],
        compiler_params=_params(("parallel", "arbitrary")),
        name=f"inproj_qkv_g{g}",
    )(h, w_in_bf16, c_tab, sa_tab, sb_tab)


def _inproj_uv(h, w_in_bf16, tm=1024):
    m, k = h.shape
    tn = ATTN_WIDTH
    first = 3 * ATTN_QKV_WIDTH // tn
    return pl.pallas_call(
        _inproj_uv_kernel,
        grid=(m // tm, _UV_TILES),
        in_specs=[pl.BlockSpec((tm, k), lambda i, j: (i, 0)),
                  pl.BlockSpec((k, tn), lambda i, j: (0, first + j))],
        out_specs=pl.BlockSpec((tm, tn), lambda i, j: (i, j)),
        out_shape=jax.ShapeDtypeStruct((m, _UV_TILES * tn), BF16),
        compiler_params=_params(("parallel", "arbitrary")),
        name="inproj_uv",
    )(h, w_in_bf16)


def _band_attn_kernel(q_ref, kp_ref, kc_ref, vp_ref, vc_ref, o_ref, lse_ref, *scratch, dil):
    i = pl.program_id(1)
    r = pl.program_id(2)
    row = lax.broadcasted_iota(jnp.int32, (BAND, BAND), 0)
    col = lax.broadcasted_iota(jnp.int32, (BAND, BAND), 1)
    mask_cur = col <= row
    mask_prev = (col >= row) & (i > 0)
    lse_tile = jnp.zeros((BAND, LANES), F32)
    dn = (((1,), (1,)), ((), ()))
    for hh in range(HEADS_PER_GROUP):
        sl = slice(hh * HEAD_DIM, (hh + 1) * HEAD_DIM)
        q = q_ref[:, sl]
        s_c = lax.dot_general(q, kc_ref[:, sl], dn, preferred_element_type=F32)
        s_p = lax.dot_general(q, kp_ref[:, sl], dn, preferred_element_type=F32)
        s_c = jnp.where(mask_cur, s_c, NEG)
        s_p = jnp.where(mask_prev, s_p, NEG)
        mx = jnp.maximum(jnp.max(s_c, axis=-1, keepdims=True), jnp.max(s_p, axis=-1, keepdims=True))
        p_c = jnp.exp(s_c - mx)
        p_p = jnp.exp(s_p - mx)
        den = jnp.sum(p_c, axis=-1, keepdims=True) + jnp.sum(p_p, axis=-1, keepdims=True)
        o = jnp.dot(p_c.astype(BF16), vc_ref[:, sl], preferred_element_type=F32)
        o = o + jnp.dot(p_p.astype(BF16), vp_ref[:, sl], preferred_element_type=F32)
        o = o / den
        if dil == 1:
            o_ref[:, sl] = o.astype(o_ref.dtype)
        else:
            scratch[0][hh, pl.ds(r, BAND, stride=dil), :] = o
        lse_tile = jnp.where(col == hh, mx + jnp.log(den), lse_tile)
    if dil == 1:
        lse_ref[...] = lse_tile
    else:
        lse_ref[pl.ds(r, BAND, stride=dil), :] = lse_tile

        @pl.when(r == dil - 1)
        def _():
            for hh in range(HEADS_PER_GROUP):
                o_ref[:, hh * HEAD_DIM:(hh + 1) * HEAD_DIM] = scratch[0][hh].astype(o_ref.dtype)


def _band_attention(qkv, g, batch, seq):
    _, dil = ATTN_GROUPS[g]
    sub = seq // dil
    nblk = sub // BAND
    rows = BAND * dil

    def spec(col, prev):
        if prev:
            return pl.BlockSpec((None, None, BAND, ATTN_WIDTH), lambda b, i, r: (b, r, jnp.maximum(i - 1, 0), col))
        return pl.BlockSpec((None, None, BAND, ATTN_WIDTH), lambda b, i, r: (b, r, i, col))

    return pl.pallas_call(
        functools.partial(_band_attn_kernel, dil=dil),
        grid=(batch, nblk, dil),
        in_specs=[spec(0, False), spec(1, True), spec(1, False), spec(2, True), spec(2, False)],
        out_specs=[pl.BlockSpec((rows, ATTN_WIDTH), lambda b, i, r: (b * nblk + i, 0)),
                   pl.BlockSpec((rows, LANES), lambda b, i, r: (b * nblk + i, 0))],
        out_shape=[jax.ShapeDtypeStruct((batch * seq, ATTN_WIDTH), BF16),
                   jax.ShapeDtypeStruct((batch * seq, LANES), F32)],
        scratch_shapes=[pltpu.VMEM((HEADS_PER_GROUP, rows, HEAD_DIM), F32)] if dil > 1 else [],
        compiler_params=_params(("parallel", "parallel", "arbitrary")),
        name=f"band_attn_g{g}",
    )(qkv, qkv, qkv, qkv, qkv)


def _combine_kernel(o0_ref, o1_ref, o2_ref, l0_ref, l1_ref, l2_ref, y_ref):
    l0, l1, l2 = l0_ref[...], l1_ref[...], l2_ref[...]
    mx = jnp.maximum(jnp.maximum(l0, l1), l2)
    e0, e1, e2 = jnp.exp(l0 - mx), jnp.exp(l1 - mx), jnp.exp(l2 - mx)
    den = e0 + e1 + e2
    a0, a1, a2 = e0 / den, e1 / den, e2 / den
    rows = l0.shape[0]
    for hh in range(HEADS_PER_GROUP):
        sl = slice(hh * HEAD_DIM, (hh + 1) * HEAD_DIM)
        b0 = jnp.broadcast_to(a0[:, hh:hh + 1], (rows, HEAD_DIM))
        b1 = jnp.broadcast_to(a1[:, hh:hh + 1], (rows, HEAD_DIM))
        b2 = jnp.broadcast_to(a2[:, hh:hh + 1], (rows, HEAD_DIM))
        y = b0 * o0_ref[:, sl].astype(F32) + b1 * o1_ref[:, sl].astype(F32) + b2 * o2_ref[:, sl].astype(F32)
        y_ref[:, sl] = y.astype(y_ref.dtype)


def _combine(os, lses, tm=512):
    m = os[0].shape[0]
    ospec = pl.BlockSpec((tm, ATTN_WIDTH), lambda i: (i, 0))
    lspec = pl.BlockSpec((tm, LANES), lambda i: (i, 0))
    return pl.pallas_call(
        _combine_kernel,
        grid=(m // tm,),
        in_specs=[ospec, ospec, ospec, lspec, lspec, lspec],
        out_specs=ospec,
        out_shape=jax.ShapeDtypeStruct((m, ATTN_WIDTH), BF16),
        compiler_params=_params(("parallel",)),
        name="attn_combine",
    )(*os, *lses)


def _sgu_kernel(u0_ref, u1_ref, v0_ref, v1_ref, g_ref, b_ref, ws_ref, bt_ref, o_ref):
    half = SGU_WIDTH // 2
    v0 = v0_ref[...].astype(F32)
    v1 = v1_ref[...].astype(F32)
    mu = (jnp.sum(v0, axis=-1, keepdims=True) + jnp.sum(v1, axis=-1, keepdims=True)) / SGU_WIDTH
    d0, d1 = v0 - mu, v1 - mu
    var = (jnp.sum(d0 * d0, axis=-1, keepdims=True) + jnp.sum(d1 * d1, axis=-1, keepdims=True)) / SGU_WIDTH
    inv = lax.rsqrt(var + EPS)
    vn = ((d0 * inv * g_ref[:, :half] + b_ref[:, :half]).astype(BF16),
          (d1 * inv * g_ref[:, half:] + b_ref[:, half:]).astype(BF16))
    u_refs = (u0_ref, u1_ref)
    row = lax.broadcasted_iota(jnp.int32, (CHUNK, CHUNK), 0)
    col = lax.broadcasted_iota(jnp.int32, (CHUNK, CHUNK), 1)
    tri = col <= row
    per_half = SGU_GROUPS // 2
    for gg in range(SGU_GROUPS):
        hf, gl = divmod(gg, per_half)
        sl = slice(gl * SGU_GROUP_CH, (gl + 1) * SGU_GROUP_CH)
        w = jnp.where(tri, ws_ref[gg], 0.0).astype(BF16)
        mixed = jnp.dot(w, vn[hf][:, sl], preferred_element_type=F32) + bt_ref[:, gg:gg + 1]
        osl = slice(gg * SGU_GROUP_CH, (gg + 1) * SGU_GROUP_CH)
        o_ref[:, osl] = (u_refs[hf][:, sl].astype(F32) * mixed).astype(o_ref.dtype)


def _sgu(proj, ln_g, ln_b, w_spatial, b_spatial):
    m = proj.shape[0]
    half = SGU_WIDTH // 2
    ucol = 0
    vec = pl.BlockSpec((1, SGU_WIDTH), lambda i: (0, 0))

    def col_spec(c):
        return pl.BlockSpec((CHUNK, half), lambda i: (i, c))

    return pl.pallas_call(
        _sgu_kernel,
        grid=(m // CHUNK,),
        in_specs=[col_spec(ucol), col_spec(ucol + 1), col_spec(ucol + 2), col_spec(ucol + 3),
                  vec, vec,
                  pl.BlockSpec((SGU_GROUPS, CHUNK, CHUNK), lambda i: (0, 0, 0)),
                  pl.BlockSpec((CHUNK, SGU_GROUPS), lambda i: (0, 0))],
        out_specs=pl.BlockSpec((CHUNK, SGU_WIDTH), lambda i: (i, 0)),
        out_shape=jax.ShapeDtypeStruct((m, SGU_WIDTH), BF16),
        compiler_params=_params(("parallel",)),
        name="sgu",
    )(proj, proj, proj, proj, ln_g.reshape(1, -1), ln_b.reshape(1, -1), w_spatial, b_spatial.T)


def _merge_kernel(h_ref, ya_ref, yb_ref, wga_ref, wgb_ref, wa_ref, wb_ref, bga_ref, bgb_ref, o_ref):
    h = h_ref[...]
    ga = jax.nn.sigmoid(jnp.dot(h, wga_ref[...], preferred_element_type=F32) + bga_ref[...])
    a = jnp.dot(ya_ref[...], wa_ref[...], preferred_element_type=F32)
    acc = ga * a
    gb = jax.nn.sigmoid(jnp.dot(h, wgb_ref[...], preferred_element_type=F32) + bgb_ref[...])
    b = jnp.dot(yb_ref[...], wb_ref[...], preferred_element_type=F32)
    o_ref[...] = (acc + gb * b).astype(o_ref.dtype)


def _merge(h, ya, yb, w_gate, b_gate, w_a, w_b, tm=512, tn=512):
    m = h.shape[0]
    nb = D_MODEL // tn
    b_gate = b_gate.reshape(1, -1)
    return pl.pallas_call(
        _merge_kernel,
        grid=(m // tm, nb),
        in_specs=[pl.BlockSpec((tm, D_MODEL), lambda i, j: (i, 0)),
                  pl.BlockSpec((tm, ATTN_WIDTH), lambda i, j: (i, 0)),
                  pl.BlockSpec((tm, SGU_WIDTH), lambda i, j: (i, 0)),
                  pl.BlockSpec((D_MODEL, tn), lambda i, j: (0, j)),
                  pl.BlockSpec((D_MODEL, tn), lambda i, j: (0, j + nb)),
                  pl.BlockSpec((ATTN_WIDTH, tn), lambda i, j: (0, j)),
                  pl.BlockSpec((SGU_WIDTH, tn), lambda i, j: (0, j)),
                  pl.BlockSpec((1, tn), lambda i, j: (0, j)),
                  pl.BlockSpec((1, tn), lambda i, j: (0, j + nb))],
        out_specs=pl.BlockSpec((tm, tn), lambda i, j: (i, j)),
        out_shape=jax.ShapeDtypeStruct((m, D_MODEL), BF16),
        compiler_params=_params(("parallel", "arbitrary")),
        name="gated_merge",
    )(h, ya, yb, w_gate, w_gate, w_a, w_b, b_gate, b_gate)


def _matmul_kernel(a_ref, w_ref, o_ref, *, act):
    acc = jnp.dot(a_ref[...], w_ref[...], preferred_element_type=F32)
    if act == "relu2":
        r = jnp.maximum(acc, 0.0)
        acc = r * r
    o_ref[...] = acc.astype(o_ref.dtype)


def _matmul(a, w, out_dtype, act=None, tm=1024, tn=1024, name="matmul"):
    m, k = a.shape
    n = w.shape[1]
    tm, tn = min(tm, m), min(tn, n)
    return pl.pallas_call(
        functools.partial(_matmul_kernel, act=act),
        grid=(m // tm, n // tn),
        in_specs=[pl.BlockSpec((tm, k), lambda i, j: (i, 0)), pl.BlockSpec((k, tn), lambda i, j: (0, j))],
        out_specs=pl.BlockSpec((tm, tn), lambda i, j: (i, j)),
        out_shape=jax.ShapeDtypeStruct((m, n), out_dtype),
        compiler_params=_params(("parallel", "arbitrary")),
        name=name,
    )(a, w)


def _matmul_kacc_kernel(a_ref, w_ref, o_ref, acc_ref):
    k = pl.program_id(2)

    @pl.when(k == 0)
    def _():
        acc_ref[...] = jnp.zeros_like(acc_ref)

    acc_ref[...] += jnp.dot(a_ref[...], w_ref[...], preferred_element_type=F32)

    @pl.when(k == pl.num_programs(2) - 1)
    def _():
        o_ref[...] = acc_ref[...].astype(o_ref.dtype)


def _matmul_kacc(a, w, out_dtype, tm=1024, tn=1024, tk=2048, name="matmul_kacc"):
    m, k = a.shape
    n = w.shape[1]
    return pl.pallas_call(
        _matmul_kacc_kernel,
        grid=(m // tm, n // tn, k // tk),
        in_specs=[pl.BlockSpec((tm, tk), lambda i, j, kk: (i, kk)),
                  pl.BlockSpec((tk, tn), lambda i, j, kk: (kk, j))],
        out_specs=pl.BlockSpec((tm, tn), lambda i, j, kk: (i, j)),
        out_shape=jax.ShapeDtypeStruct((m, n), out_dtype),
        scratch_shapes=[pltpu.VMEM((tm, tn), F32)],
        compiler_params=_params(("parallel", "parallel", "arbitrary")),
        name=name,
    )(a, w)


def _norm_residual_kernel(y_ref, x_ref, g_ref, gn_ref, xo_ref, hn_ref):
    y = y_ref[...]
    x_new = x_ref[...] + y * lax.rsqrt(jnp.mean(y * y, axis=-1, keepdims=True) + EPS) * g_ref[...]
    xo_ref[...] = x_new
    hn = x_new * lax.rsqrt(jnp.mean(x_new * x_new, axis=-1, keepdims=True) + EPS) * gn_ref[...]
    hn_ref[...] = hn.astype(hn_ref.dtype)


def _norm_residual_last_kernel(y_ref, x_ref, g_ref, xo_ref):
    y = y_ref[...]
    xo_ref[...] = x_ref[...] + y * lax.rsqrt(jnp.mean(y * y, axis=-1, keepdims=True) + EPS) * g_ref[...]


def _norm_residual(y, x, g, g_next=None, tm=256):
    m, d = y.shape
    row = pl.BlockSpec((tm, d), lambda i: (i, 0))
    vec = pl.BlockSpec((1, d), lambda i: (0, 0))
    if g_next is None:
        return pl.pallas_call(
            _norm_residual_last_kernel,
            grid=(m // tm,),
            in_specs=[row, row, vec],
            out_specs=row,
            out_shape=jax.ShapeDtypeStruct((m, d), F32),
            compiler_params=_params(("parallel",)),
            name="norm_residual_last",
        )(y, x, g.reshape(1, d))
    return pl.pallas_call(
        _norm_residual_kernel,
        grid=(m // tm,),
        in_specs=[row, row, vec, vec],
        out_specs=[row, row],
        out_shape=[jax.ShapeDtypeStruct((m, d), F32), jax.ShapeDtypeStruct((m, d), BF16)],
        compiler_params=_params(("parallel",)),
        name="norm_residual",
    )(y, x, g.reshape(1, d), g_next.reshape(1, d))


def _xattn_kernel(h_ref, wq_ref, k_ref, v_ref, wo_ref, o_ref):
    q = jnp.dot(h_ref[...], wq_ref[...], preferred_element_type=F32) * ATTN_SCALE
    q = q.astype(BF16)
    dn = (((1,), (1,)), ((), ()))
    outs = []
    for hh in range(XA_HEADS):
        sl = slice(hh * HEAD_DIM, (hh + 1) * HEAD_DIM)
        s = lax.dot_general(q[:, sl], k_ref[:, sl], dn, preferred_element_type=F32)
        mx = jnp.max(s, axis=-1, keepdims=True)
        p = jnp.exp(s - mx)
        den = jnp.sum(p, axis=-1, keepdims=True)
        o = jnp.dot(p.astype(BF16), v_ref[:, sl], preferred_element_type=F32)
        outs.append((o / den).astype(BF16))
    o_all = jnp.concatenate(outs, axis=-1)
    o_ref[...] = jnp.dot(o_all, wo_ref[...], preferred_element_type=F32)


def _xattn(h, w_xq, k_mem, v_mem, w_xo, batch, seq, tm=512):
    m = h.shape[0]
    n_mem = k_mem.shape[0] // batch
    per_b = seq // tm
    return pl.pallas_call(
        _xattn_kernel,
        grid=(m // tm,),
        in_specs=[pl.BlockSpec((tm, D_MODEL), lambda i: (i, 0)),
                  pl.BlockSpec((D_MODEL, XA_WIDTH), lambda i: (0, 0)),
                  pl.BlockSpec((n_mem, XA_WIDTH), lambda i: (i // per_b, 0)),
                  pl.BlockSpec((n_mem, XA_WIDTH), lambda i: (i // per_b, 0)),
                  pl.BlockSpec((XA_WIDTH, D_MODEL), lambda i: (0, 0))],
        out_specs=pl.BlockSpec((tm, D_MODEL), lambda i: (i, 0)),
        out_shape=jax.ShapeDtypeStruct((m, D_MODEL), F32),
        compiler_params=_params(("parallel",)),
        name="cross_attn",
    )(h, w_xq, k_mem, v_mem, w_xo)


def kernel(x, mem, positions, mix_pre_g, w_in, sgu_ln_g, sgu_ln_b, w_spatial, b_spatial, w_branch_a, w_branch_b, w_gate, b_gate, w_out, mix_post_g, xa_pre_g, mem_norm_g, w_xq, w_xk, w_xv, w_xo, xa_post_g, mlp_pre_g, w_up, w_down, mlp_post_g):
    batch, seq, d = x.shape
    n_mem = mem.shape[1]
    depth = w_in.shape[0]
    m = batch * seq
    x2 = x.reshape(m, d)
    mem2 = mem.reshape(batch * n_mem, d)
    c_tab, sa_tab, sb_tab = _rope_tables(positions)

    for l in range(depth):
        bf = lambda w: w[l].astype(BF16)
        h = _rmsnorm(x2, mix_pre_g[l])
        w_in_bf = bf(w_in)
        uv = _inproj_uv(h, w_in_bf)
        os, lses = [], []
        for g in range(N_GROUPS):
            qkv = _inproj_qkv(h, w_in_bf, c_tab, sa_tab, sb_tab, g, batch, seq)
            o, lse = _band_attention(qkv, g, batch, seq)
            os.append(o)
            lses.append(lse)
        y_a = _combine(os, lses)
        y_b = _sgu(uv, sgu_ln_g[l], sgu_ln_b[l], w_spatial[l], b_spatial[l])
        merged = _merge(h, y_a, y_b, bf(w_gate), b_gate[l], bf(w_branch_a), bf(w_branch_b))
        y = _matmul(merged, bf(w_out), F32, name="out_proj")
        x2, h = _norm_residual(y, x2, mix_post_g[l], xa_pre_g[l])

        mn = _rmsnorm(mem2, mem_norm_g[l])
        k_mem = _matmul(mn, bf(w_xk), BF16, name="mem_k")
        v_mem = _matmul(mn, bf(w_xv), BF16, name="mem_v")
        y = _xattn(h, bf(w_xq), k_mem, v_mem, bf(w_xo), batch, seq)
        x2, h = _norm_residual(y, x2, xa_post_g[l], mlp_pre_g[l])

        a = _matmul(h, bf(w_up), BF16, act="relu2", name="mlp_up")
        y = _matmul_kacc(a, bf(w_down), F32, name="mlp_down")
        if l + 1 < depth:
            x2, h = _norm_residual(y, x2, mlp_post_g[l], mix_pre_g[l + 1])
        else:
            x2 = _norm_residual(y, x2, mlp_post_g[l])
    return x2.reshape(batch, seq, d)
```

```python
import functools
import math

import jax
import jax.numpy as jnp
from jax import lax
from jax.experimental import pallas as pl
from jax.experimental.pallas import tpu as pltpu

F32 = jnp.float32
BF16 = jnp.bfloat16

D_MODEL = 4096
HEAD_DIM = 128
ATTN_GROUPS = ((128, 1), (512, 4), (2048, 16))
N_GROUPS = len(ATTN_GROUPS)
HEADS_PER_GROUP = 8
ATTN_WIDTH = HEADS_PER_GROUP * HEAD_DIM
ATTN_QKV_WIDTH = N_GROUPS * ATTN_WIDTH
BAND = 128
SGU_WIDTH = D_MODEL // 2
SGU_GROUP_CH = 128
SGU_GROUPS = SGU_WIDTH // SGU_GROUP_CH
CHUNK = 128
IN_WIDTH = 3 * ATTN_QKV_WIDTH + 2 * SGU_WIDTH
ROPE_THETA = 500000.0
ROT_DIM = HEAD_DIM // 4
ROT_HALF = ROT_DIM // 2
XA_HEADS = 4
XA_WIDTH = XA_HEADS * HEAD_DIM
D_FF = 4 * D_MODEL
EPS = 1e-6
NEG = -1e30
ATTN_SCALE = HEAD_DIM ** -0.5
LANES = 128
BF16_SUBLANES = 16

VMEM_LIMIT = 56 * 1024 * 1024


def _params(sem, vmem=VMEM_LIMIT):
    return pltpu.CompilerParams(dimension_semantics=sem, vmem_limit_bytes=vmem)


def _gelu_tanh(x):
    c = math.sqrt(2.0 / math.pi)
    return 0.5 * x * (1.0 + jnp.tanh(c * (x + 0.044715 * (x * x * x))))


def _rms(v, g):
    return v * lax.rsqrt(jnp.mean(v * v, axis=-1, keepdims=True) + EPS) * g


def _side_cast_specs(w, grid):
    rows, cols = w.shape
    steps = grid[0] * grid[1]
    nblk = steps
    while rows % nblk or (rows // nblk) % BF16_SUBLANES:
        nblk -= 1

    def imap(i, j):
        return jnp.minimum(i * grid[1] + j, nblk - 1), 0

    spec = pl.BlockSpec((rows // nblk, cols), imap)
    return spec, spec, jax.ShapeDtypeStruct((rows, cols), BF16)


def _rope_table_kernel(pos_ref, inv_ref, c_ref, sa_ref, sb_ref):
    ang = pos_ref[...] * inv_ref[...]
    c = jnp.cos(ang)
    s = jnp.sin(ang)
    lane = lax.broadcasted_iota(jnp.int32, ang.shape, 1)
    c_ref[...] = jnp.where(lane < ROT_DIM, c, 1.0)
    sa_ref[...] = jnp.where(lane < ROT_HALF, -s, 0.0)
    sb_ref[...] = jnp.where((lane >= ROT_HALF) & (lane < ROT_DIM), s, 0.0)


def _rope_tables(positions):
    m = positions.size
    tm = 2048
    inv = ROPE_THETA ** (-jnp.arange(0, ROT_DIM, 2, dtype=F32) / ROT_DIM)
    inv_lane = jnp.concatenate([inv, inv, jnp.zeros((LANES - ROT_DIM,), F32)])[None, :]
    pos = positions.astype(F32).reshape(m, 1)
    out = jax.ShapeDtypeStruct((m, LANES), F32)
    spec = pl.BlockSpec((tm, LANES), lambda i: (i, 0))
    return pl.pallas_call(
        _rope_table_kernel,
        grid=(m // tm,),
        in_specs=[pl.BlockSpec((tm, 1), lambda i: (i, 0)), pl.BlockSpec((1, LANES), lambda i: (0, 0))],
        out_specs=[spec, spec, spec],
        out_shape=[out, out, out],
        compiler_params=_params(("parallel",)),
        name="rope_tables",
    )(pos, inv_lane)


def _rmsnorm_kernel(x_ref, g_ref, o_ref):
    o_ref[...] = _rms(x_ref[...], g_ref[...]).astype(o_ref.dtype)


def _rmsnorm(x2d, g, out_dtype=BF16, tm=512):
    m, d = x2d.shape
    return pl.pallas_call(
        _rmsnorm_kernel,
        grid=(m // tm,),
        in_specs=[pl.BlockSpec((tm, d), lambda i: (i, 0)), pl.BlockSpec((1, d), lambda i: (0, 0))],
        out_specs=pl.BlockSpec((tm, d), lambda i: (i, 0)),
        out_shape=jax.ShapeDtypeStruct((m, d), out_dtype),
        compiler_params=_params(("parallel",)),
        name="rmsnorm",
    )(x2d, g.reshape(1, d))


_UV_TILES = 2 * SGU_WIDTH // ATTN_WIDTH
_DOT_CHUNK = 2 * HEAD_DIM


def _inproj_qkv_kernel(h_ref, w_ref, c_ref, sa_ref, sb_ref, o_ref, *scratch, dil):
    kind = pl.program_id(1)
    tm = h_ref.shape[0]
    c = c_ref[...]
    sa = sa_ref[...]
    sb = sb_ref[...]
    rotate = kind < 2
    scale = jnp.where(kind == 0, ATTN_SCALE, 1.0).astype(F32)
    h = h_ref[...]
    for cc in range(ATTN_WIDTH // _DOT_CHUNK):
        acc = jnp.dot(h, w_ref[:, cc * _DOT_CHUNK:(cc + 1) * _DOT_CHUNK], preferred_element_type=F32)
        for hc in range(_DOT_CHUNK // HEAD_DIM):
            hh = cc * (_DOT_CHUNK // HEAD_DIM) + hc
            xh = acc[:, hc * HEAD_DIM:(hc + 1) * HEAD_DIM]
            y = xh * c + pltpu.roll(xh, LANES - ROT_HALF, 1) * sa + pltpu.roll(xh, ROT_HALF, 1) * sb
            y = jnp.where(rotate, y * scale, xh)
            if dil == 1:
                o_ref[0, :, hh * HEAD_DIM:(hh + 1) * HEAD_DIM] = y.astype(o_ref.dtype)
            else:
                scratch[0][hh] = y
    if dil > 1:
        for hh in range(HEADS_PER_GROUP):
            for r in range(dil):
                rows = scratch[0][hh, pl.ds(r, tm // dil, stride=dil), :]
                o_ref[r, :, hh * HEAD_DIM:(hh + 1) * HEAD_DIM] = rows.astype(o_ref.dtype)


def _inproj_uv_kernel(h_ref, w_ref, cast_ref, o_ref, cast_o_ref):
    cast_o_ref[...] = cast_ref[...].astype(cast_o_ref.dtype)
    h = h_ref[...]
    for cc in range(ATTN_WIDTH // _DOT_CHUNK):
        sl = slice(cc * _DOT_CHUNK, (cc + 1) * _DOT_CHUNK)
        acc = jnp.dot(h, w_ref[:, sl], preferred_element_type=F32)
        o_ref[:, sl] = _gelu_tanh(acc).astype(o_ref.dtype)


def _inproj_qkv(h, w_in_bf16, c_tab, sa_tab, sb_tab, g, batch, seq, tm=1024):
    m, k = h.shape
    tn = ATTN_WIDTH
    dil = ATTN_GROUPS[g][1]
    per_b = seq // tm
    tab_spec = pl.BlockSpec((tm, LANES), lambda i, kind: (i, 0))
    return pl.pallas_call(
        functools.partial(_inproj_qkv_kernel, dil=dil),
        grid=(m // tm, 3),
        in_specs=[pl.BlockSpec((tm, k), lambda i, kind: (i, 0)),
                  pl.BlockSpec((k, tn), lambda i, kind: (0, kind * N_GROUPS + g)),
                  tab_spec, tab_spec, tab_spec],
        out_specs=pl.BlockSpec((None, dil, tm // dil, tn), lambda i, kind: (i // per_b, 0, i % per_b, kind)),
        out_shape=jax.ShapeDtypeStruct((batch, dil, seq // dil, 3 * tn), BF16),
        scratch_shapes=[pltpu.VMEM((HEADS_PER_GROUP, tm, HEAD_DIM), F32)] if dil > 1 else [],
        compiler_params=_params(("parallel", "arbitrary")),
        name=f"inproj_qkv_g{g}",
    )(h, w_in_bf16, c_tab, sa_tab, sb_tab)


def _inproj_uv(h, w_in_bf16, cast_w, tm=1024):
    m, k = h.shape
    tn = ATTN_WIDTH
    first = 3 * ATTN_QKV_WIDTH // tn
    grid = (m // tm, _UV_TILES)
    cin, cout, cshape = _side_cast_specs(cast_w, grid)
    return pl.pallas_call(
        _inproj_uv_kernel,
        grid=grid,
        in_specs=[pl.BlockSpec((tm, k), lambda i, j: (i, 0)),
                  pl.BlockSpec((k, tn), lambda i, j: (0, first + j)),
                  cin],
        out_specs=[pl.BlockSpec((tm, tn), lambda i, j: (i, j)), cout],
        out_shape=[jax.ShapeDtypeStruct((m, _UV_TILES * tn), BF16), cshape],
        compiler_params=_params(("parallel", "arbitrary")),
        name="inproj_uv",
    )(h, w_in_bf16, cast_w)


def _band_attn_kernel(q_ref, kp_ref, kc_ref, vp_ref, vc_ref, o_ref, lse_ref, *scratch, dil):
    i = pl.program_id(1)
    r = pl.program_id(2)
    row = lax.broadcasted_iota(jnp.int32, (BAND, BAND), 0)
    col = lax.broadcasted_iota(jnp.int32, (BAND, BAND), 1)
    mask_cur = col <= row
    mask_prev = (col >= row) & (i > 0)

    def heads(ref):
        return jnp.stack([ref[:, hh * HEAD_DIM:(hh + 1) * HEAD_DIM] for hh in range(HEADS_PER_GROUP)])

    q, k_c, k_p, v_c, v_p = heads(q_ref), heads(kc_ref), heads(kp_ref), heads(vc_ref), heads(vp_ref)
    s_c = jnp.einsum("hqd,hkd->hqk", q, k_c, preferred_element_type=F32)
    s_p = jnp.einsum("hqd,hkd->hqk", q, k_p, preferred_element_type=F32)
    s_c = jnp.where(mask_cur, s_c, NEG)
    s_p = jnp.where(mask_prev, s_p, NEG)
    mx = jnp.maximum(jnp.max(s_c, axis=-1, keepdims=True), jnp.max(s_p, axis=-1, keepdims=True))
    p_c = jnp.exp(s_c - mx)
    p_p = jnp.exp(s_p - mx)
    den = jnp.sum(p_c, axis=-1, keepdims=True) + jnp.sum(p_p, axis=-1, keepdims=True)
    o = jnp.einsum("hqk,hkd->hqd", p_c.astype(BF16), v_c, preferred_element_type=F32)
    o = o + jnp.einsum("hqk,hkd->hqd", p_p.astype(BF16), v_p, preferred_element_type=F32)
    o = o / den
    lse = mx + jnp.log(den)
    lse_tile = jnp.zeros((BAND, LANES), F32)
    for hh in range(HEADS_PER_GROUP):
        if dil == 1:
            o_ref[:, hh * HEAD_DIM:(hh + 1) * HEAD_DIM] = o[hh].astype(o_ref.dtype)
        else:
            scratch[0][hh, pl.ds(r, BAND, stride=dil), :] = o[hh]
        lse_tile = jnp.where(col == hh, lse[hh], lse_tile)
    if dil == 1:
        lse_ref[...] = lse_tile
    else:
        lse_ref[pl.ds(r, BAND, stride=dil), :] = lse_tile

        @pl.when(r == dil - 1)
        def _():
            for hh in range(HEADS_PER_GROUP):
                o_ref[:, hh * HEAD_DIM:(hh + 1) * HEAD_DIM] = scratch[0][hh].astype(o_ref.dtype)


def _band_attention(qkv, g, batch, seq):
    _, dil = ATTN_GROUPS[g]
    sub = seq // dil
    nblk = sub // BAND
    rows = BAND * dil

    def spec(col, prev):
        if prev:
            return pl.BlockSpec((None, None, BAND, ATTN_WIDTH), lambda b, i, r: (b, r, jnp.maximum(i - 1, 0), col))
        return pl.BlockSpec((None, None, BAND, ATTN_WIDTH), lambda b, i, r: (b, r, i, col))

    return pl.pallas_call(
        functools.partial(_band_attn_kernel, dil=dil),
        grid=(batch, nblk, dil),
        in_specs=[spec(0, False), spec(1, True), spec(1, False), spec(2, True), spec(2, False)],
        out_specs=[pl.BlockSpec((rows, ATTN_WIDTH), lambda b, i, r: (b * nblk + i, 0)),
                   pl.BlockSpec((rows, LANES), lambda b, i, r: (b * nblk + i, 0))],
        out_shape=[jax.ShapeDtypeStruct((batch * seq, ATTN_WIDTH), BF16),
                   jax.ShapeDtypeStruct((batch * seq, LANES), F32)],
        scratch_shapes=[pltpu.VMEM((HEADS_PER_GROUP, rows, HEAD_DIM), F32)] if dil > 1 else [],
        compiler_params=_params(("parallel", "parallel", "arbitrary")),
        name=f"band_attn_g{g}",
    )(qkv, qkv, qkv, qkv, qkv)


def _combine_kernel(o0_ref, o1_ref, o2_ref, l0_ref, l1_ref, l2_ref, y_ref):
    l0, l1, l2 = l0_ref[...], l1_ref[...], l2_ref[...]
    mx = jnp.maximum(jnp.maximum(l0, l1), l2)
    e0, e1, e2 = jnp.exp(l0 - mx), jnp.exp(l1 - mx), jnp.exp(l2 - mx)
    den = e0 + e1 + e2
    a0, a1, a2 = e0 / den, e1 / den, e2 / den
    rows = l0.shape[0]
    for hh in range(HEADS_PER_GROUP):
        sl = slice(hh * HEAD_DIM, (hh + 1) * HEAD_DIM)
        b0 = jnp.broadcast_to(a0[:, hh:hh + 1], (rows, HEAD_DIM))
        b1 = jnp.broadcast_to(a1[:, hh:hh + 1], (rows, HEAD_DIM))
        b2 = jnp.broadcast_to(a2[:, hh:hh + 1], (rows, HEAD_DIM))
        y = b0 * o0_ref[:, sl].astype(F32) + b1 * o1_ref[:, sl].astype(F32) + b2 * o2_ref[:, sl].astype(F32)
        y_ref[:, sl] = y.astype(y_ref.dtype)


def _combine(os, lses, tm=512):
    m = os[0].shape[0]
    ospec = pl.BlockSpec((tm, ATTN_WIDTH), lambda i: (i, 0))
    lspec = pl.BlockSpec((tm, LANES), lambda i: (i, 0))
    return pl.pallas_call(
        _combine_kernel,
        grid=(m // tm,),
        in_specs=[ospec, ospec, ospec, lspec, lspec, lspec],
        out_specs=ospec,
        out_shape=jax.ShapeDtypeStruct((m, ATTN_WIDTH), BF16),
        compiler_params=_params(("parallel",)),
        name="attn_combine",
    )(*os, *lses)


def _sgu_kernel(u0_ref, u1_ref, v0_ref, v1_ref, g_ref, b_ref, ws_ref, bt_ref, o_ref):
    half = SGU_WIDTH // 2
    v0 = v0_ref[...].astype(F32)
    v1 = v1_ref[...].astype(F32)
    mu = (jnp.sum(v0, axis=-1, keepdims=True) + jnp.sum(v1, axis=-1, keepdims=True)) / SGU_WIDTH
    d0, d1 = v0 - mu, v1 - mu
    var = (jnp.sum(d0 * d0, axis=-1, keepdims=True) + jnp.sum(d1 * d1, axis=-1, keepdims=True)) / SGU_WIDTH
    inv = lax.rsqrt(var + EPS)
    vn = ((d0 * inv * g_ref[:, :half] + b_ref[:, :half]).astype(BF16),
          (d1 * inv * g_ref[:, half:] + b_ref[:, half:]).astype(BF16))
    u_refs = (u0_ref, u1_ref)
    row = lax.broadcasted_iota(jnp.int32, (CHUNK, CHUNK), 0)
    col = lax.broadcasted_iota(jnp.int32, (CHUNK, CHUNK), 1)
    tri = col <= row
    per_half = SGU_GROUPS // 2
    for gg in range(SGU_GROUPS):
        hf, gl = divmod(gg, per_half)
        sl = slice(gl * SGU_GROUP_CH, (gl + 1) * SGU_GROUP_CH)
        w = jnp.where(tri, ws_ref[gg], 0.0).astype(BF16)
        mixed = jnp.dot(w, vn[hf][:, sl], preferred_element_type=F32) + bt_ref[:, gg:gg + 1]
        osl = slice(gg * SGU_GROUP_CH, (gg + 1) * SGU_GROUP_CH)
        o_ref[:, osl] = (u_refs[hf][:, sl].astype(F32) * mixed).astype(o_ref.dtype)


def _sgu(proj, ln_g, ln_b, w_spatial, b_spatial):
    m = proj.shape[0]
    half = SGU_WIDTH // 2
    vec = pl.BlockSpec((1, SGU_WIDTH), lambda i: (0, 0))

    def col_spec(c):
        return pl.BlockSpec((CHUNK, half), lambda i: (i, c))

    return pl.pallas_call(
        _sgu_kernel,
        grid=(m // CHUNK,),
        in_specs=[col_spec(0), col_spec(1), col_spec(2), col_spec(3),
                  vec, vec,
                  pl.BlockSpec((SGU_GROUPS, CHUNK, CHUNK), lambda i: (0, 0, 0)),
                  pl.BlockSpec((CHUNK, SGU_GROUPS), lambda i: (0, 0))],
        out_specs=pl.BlockSpec((CHUNK, SGU_WIDTH), lambda i: (i, 0)),
        out_shape=jax.ShapeDtypeStruct((m, SGU_WIDTH), BF16),
        compiler_params=_params(("parallel",)),
        name="sgu",
    )(proj, proj, proj, proj, ln_g.reshape(1, -1), ln_b.reshape(1, -1), w_spatial, b_spatial.T)


def _merge_kernel(h_ref, ya_ref, yb_ref, wga_ref, wgb_ref, wa_ref, wb_ref, bga_ref, bgb_ref, cast_ref,
                  o_ref, cast_o_ref):
    cast_o_ref[...] = cast_ref[...].astype(cast_o_ref.dtype)
    h = h_ref[...]
    ga = jax.nn.sigmoid(jnp.dot(h, wga_ref[...], preferred_element_type=F32) + bga_ref[...])
    a = jnp.dot(ya_ref[...], wa_ref[...], preferred_element_type=F32)
    acc = ga * a
    gb = jax.nn.sigmoid(jnp.dot(h, wgb_ref[...], preferred_element_type=F32) + bgb_ref[...])
    b = jnp.dot(yb_ref[...], wb_ref[...], preferred_element_type=F32)
    o_ref[...] = (acc + gb * b).astype(o_ref.dtype)


def _merge(h, ya, yb, w_gate, b_gate, w_a, w_b, cast_w, tm=512, tn=512):
    m = h.shape[0]
    nb = D_MODEL // tn
    b_gate = b_gate.reshape(1, -1)
    grid = (m // tm, nb)
    cin, cout, cshape = _side_cast_specs(cast_w, grid)
    return pl.pallas_call(
        _merge_kernel,
        grid=grid,
        in_specs=[pl.BlockSpec((tm, D_MODEL), lambda i, j: (i, 0)),
                  pl.BlockSpec((tm, ATTN_WIDTH), lambda i, j: (i, 0)),
                  pl.BlockSpec((tm, SGU_WIDTH), lambda i, j: (i, 0)),
                  pl.BlockSpec((D_MODEL, tn), lambda i, j: (0, j)),
                  pl.BlockSpec((D_MODEL, tn), lambda i, j: (0, j + nb)),
                  pl.BlockSpec((ATTN_WIDTH, tn), lambda i, j: (0, j)),
                  pl.BlockSpec((SGU_WIDTH, tn), lambda i, j: (0, j)),
                  pl.BlockSpec((1, tn), lambda i, j: (0, j)),
                  pl.BlockSpec((1, tn), lambda i, j: (0, j + nb)),
                  cin],
        out_specs=[pl.BlockSpec((tm, tn), lambda i, j: (i, j)), cout],
        out_shape=[jax.ShapeDtypeStruct((m, D_MODEL), BF16), cshape],
        compiler_params=_params(("parallel", "arbitrary")),
        name="gated_merge",
    )(h, ya, yb, w_gate, w_gate, w_a, w_b, b_gate, b_gate, cast_w)


def _matmul_kernel(a_ref, w_ref, *rest, act):
    o_ref = rest[-1] if len(rest) == 1 else rest[1]
    acc = jnp.dot(a_ref[...], w_ref[...], preferred_element_type=F32)
    if act == "relu2":
        r = jnp.maximum(acc, 0.0)
        acc = r * r
    o_ref[...] = acc.astype(o_ref.dtype)
    if len(rest) == 3:
        rest[2][...] = rest[0][...].astype(rest[2].dtype)


def _matmul(a, w, out_dtype, act=None, tm=1024, tn=1024, name="matmul", cast_w=None):
    m, k = a.shape
    n = w.shape[1]
    tm, tn = min(tm, m), min(tn, n)
    grid = (m // tm, n // tn)
    in_specs = [pl.BlockSpec((tm, k), lambda i, j: (i, 0)), pl.BlockSpec((k, tn), lambda i, j: (0, j))]
    out_specs = [pl.BlockSpec((tm, tn), lambda i, j: (i, j))]
    out_shape = [jax.ShapeDtypeStruct((m, n), out_dtype)]
    args = [a, w]
    if cast_w is not None:
        cin, cout, cshape = _side_cast_specs(cast_w, grid)
        in_specs.append(cin)
        out_specs.append(cout)
        out_shape.append(cshape)
        args.append(cast_w)
    res = pl.pallas_call(
        functools.partial(_matmul_kernel, act=act),
        grid=grid,
        in_specs=in_specs,
        out_specs=out_specs,
        out_shape=out_shape,
        compiler_params=_params(("parallel", "arbitrary")),
        name=name,
    )(*args)
    return res[0] if cast_w is None else res


def _matmul_kacc_kernel(a_ref, w_ref, o_ref, acc_ref):
    k = pl.program_id(2)

    @pl.when(k == 0)
    def _():
        acc_ref[...] = jnp.zeros_like(acc_ref)

    acc_ref[...] += jnp.dot(a_ref[...], w_ref[...], preferred_element_type=F32)

    @pl.when(k == pl.num_programs(2) - 1)
    def _():
        o_ref[...] = acc_ref[...].astype(o_ref.dtype)


def _matmul_kacc(a, w, out_dtype, tm=1024, tn=1024, tk=4096, name="matmul_kacc"):
    m, k = a.shape
    n = w.shape[1]
    return pl.pallas_call(
        _matmul_kacc_kernel,
        grid=(m // tm, n // tn, k // tk),
        in_specs=[pl.BlockSpec((tm, tk), lambda i, j, kk: (i, kk)),
                  pl.BlockSpec((tk, tn), lambda i, j, kk: (kk, j))],
        out_specs=pl.BlockSpec((tm, tn), lambda i, j, kk: (i, j)),
        out_shape=jax.ShapeDtypeStruct((m, n), out_dtype),
        scratch_shapes=[pltpu.VMEM((tm, tn), F32)],
        compiler_params=_params(("parallel", "parallel", "arbitrary")),
        name=name,
    )(a, w)


def _proj_norm_kernel(a_ref, w_ref, x_ref, g_ref, xo_ref, y_ref):
    j = pl.program_id(1)
    tn = w_ref.shape[1]
    col = pl.multiple_of(j * tn, tn)
    y_ref[:, pl.ds(col, tn)] = jnp.dot(a_ref[...], w_ref[...], preferred_element_type=F32)
    xo_ref[:, pl.ds(col, tn)] = x_ref[...]

    @pl.when(j == pl.num_programs(1) - 1)
    def _():
        xo_ref[...] = xo_ref[...] + _rms(y_ref[...], g_ref[...])


def _proj_norm_residual(a, w, x, g, tm=512, tn=512):
    m, k = a.shape
    d = w.shape[1]
    return pl.pallas_call(
        _proj_norm_kernel,
        grid=(m // tm, d // tn),
        in_specs=[pl.BlockSpec((tm, k), lambda i, j: (i, 0)),
                  pl.BlockSpec((k, tn), lambda i, j: (0, j)),
                  pl.BlockSpec((tm, tn), lambda i, j: (i, j)),
                  pl.BlockSpec((1, d), lambda i, j: (0, 0))],
        out_specs=pl.BlockSpec((tm, d), lambda i, j: (i, 0)),
        out_shape=jax.ShapeDtypeStruct((m, d), F32),
        scratch_shapes=[pltpu.VMEM((tm, d), F32)],
        compiler_params=_params(("parallel", "arbitrary")),
        name="out_proj_norm",
    )(a, w, x, g.reshape(1, d))


def _norm_residual_kernel(y_ref, x_ref, g_ref, xo_ref):
    xo_ref[...] = x_ref[...] + _rms(y_ref[...], g_ref[...])


def _norm_residual(y, x, g, tm=256):
    m, d = y.shape
    row = pl.BlockSpec((tm, d), lambda i: (i, 0))
    return pl.pallas_call(
        _norm_residual_kernel,
        grid=(m // tm,),
        in_specs=[row, row, pl.BlockSpec((1, d), lambda i: (0, 0))],
        out_specs=row,
        out_shape=jax.ShapeDtypeStruct((m, d), F32),
        compiler_params=_params(("parallel",)),
        name="norm_residual",
    )(y, x, g.reshape(1, d))


def _xattn_kernel(x_ref, gpre_ref, wq_ref, k_ref, v_ref, wo_ref, gpost_ref, gnext_ref, xo_ref, hn_ref):
    x = x_ref[...]
    h = _rms(x, gpre_ref[...]).astype(BF16)
    q = jnp.dot(h, wq_ref[...], preferred_element_type=F32) * ATTN_SCALE
    q = q.astype(BF16)
    dn = (((1,), (1,)), ((), ()))
    outs = []
    for hh in range(XA_HEADS):
        sl = slice(hh * HEAD_DIM, (hh + 1) * HEAD_DIM)
        s = lax.dot_general(q[:, sl], k_ref[:, sl], dn, preferred_element_type=F32)
        mx = jnp.max(s, axis=-1, keepdims=True)
        p = jnp.exp(s - mx)
        den = jnp.sum(p, axis=-1, keepdims=True)
        o = jnp.dot(p.astype(BF16), v_ref[:, sl], preferred_element_type=F32)
        outs.append((o / den).astype(BF16))
    o_all = jnp.concatenate(outs, axis=-1)
    y = jnp.dot(o_all, wo_ref[...], preferred_element_type=F32)
    x_new = x + _rms(y, gpost_ref[...])
    xo_ref[...] = x_new
    hn_ref[...] = _rms(x_new, gnext_ref[...]).astype(hn_ref.dtype)


def _xattn(x, g_pre, w_xq, k_mem, v_mem, w_xo, g_post, g_next, batch, seq, tm=256):
    m, d = x.shape
    n_mem = k_mem.shape[0] // batch
    per_b = seq // tm
    row = pl.BlockSpec((tm, d), lambda i: (i, 0))
    vec = pl.BlockSpec((1, d), lambda i: (0, 0))
    return pl.pallas_call(
        _xattn_kernel,
        grid=(m // tm,),
        in_specs=[row, vec,
                  pl.BlockSpec((d, XA_WIDTH), lambda i: (0, 0)),
                  pl.BlockSpec((n_mem, XA_WIDTH), lambda i: (i // per_b, 0)),
                  pl.BlockSpec((n_mem, XA_WIDTH), lambda i: (i // per_b, 0)),
                  pl.BlockSpec((XA_WIDTH, d), lambda i: (0, 0)),
                  vec, vec],
        out_specs=[row, row],
        out_shape=[jax.ShapeDtypeStruct((m, d), F32), jax.ShapeDtypeStruct((m, d), BF16)],
        compiler_params=_params(("parallel",)),
        name="cross_attn",
    )(x, g_pre.reshape(1, d), w_xq, k_mem, v_mem, w_xo, g_post.reshape(1, d), g_next.reshape(1, d))


def kernel(x, mem, positions, mix_pre_g, w_in, sgu_ln_g, sgu_ln_b, w_spatial, b_spatial, w_branch_a, w_branch_b, w_gate, b_gate, w_out, mix_post_g, xa_pre_g, mem_norm_g, w_xq, w_xk, w_xv, w_xo, xa_post_g, mlp_pre_g, w_up, w_down, mlp_post_g):
    batch, seq, d = x.shape
    n_mem = mem.shape[1]
    depth = w_in.shape[0]
    m = batch * seq
    x2 = x.reshape(m, d)
    mem2 = mem.reshape(batch * n_mem, d)
    c_tab, sa_tab, sb_tab = _rope_tables(positions)

    for l in range(depth):
        bf = lambda w: w[l].astype(BF16)
        h = _rmsnorm(x2, mix_pre_g[l])
        w_in_bf = bf(w_in)
        uv, w_gate_bf = _inproj_uv(h, w_in_bf, w_gate[l])
        os, lses = [], []
        for g in range(N_GROUPS):
            qkv = _inproj_qkv(h, w_in_bf, c_tab, sa_tab, sb_tab, g, batch, seq)
            o, lse = _band_attention(qkv, g, batch, seq)
            os.append(o)
            lses.append(lse)
        y_a = _combine(os, lses)
        y_b = _sgu(uv, sgu_ln_g[l], sgu_ln_b[l], w_spatial[l], b_spatial[l])
        merged, w_up_bf = _merge(h, y_a, y_b, w_gate_bf, b_gate[l], bf(w_branch_a), bf(w_branch_b), w_up[l])
        x2 = _proj_norm_residual(merged, bf(w_out), x2, mix_post_g[l])

        mn = _rmsnorm(mem2, mem_norm_g[l])
        k_mem = _matmul(mn, bf(w_xk), BF16, name="mem_k")
        v_mem = _matmul(mn, bf(w_xv), BF16, name="mem_v")
        x2, h = _xattn(x2, xa_pre_g[l], bf(w_xq), k_mem, v_mem, bf(w_xo), xa_post_g[l], mlp_pre_g[l], batch, seq)

        a, w_down_bf = _matmul(h, w_up_bf, BF16, act="relu2", name="mlp_up", cast_w=w_down[l])
        y = _matmul_kacc(a, w_down_bf, F32, name="mlp_down")
        x2 = _norm_residual(y, x2, mlp_post_g[l])
    return x2.reshape(batch, seq, d)
```

```python
import functools
import math

import jax
import jax.numpy as jnp
from jax import lax
from jax.experimental import pallas as pl
from jax.experimental.pallas import tpu as pltpu

F32 = jnp.float32
BF16 = jnp.bfloat16

D_MODEL = 4096
HEAD_DIM = 128
ATTN_GROUPS = ((128, 1), (512, 4), (2048, 16))
N_GROUPS = len(ATTN_GROUPS)
HEADS_PER_GROUP = 8
ATTN_WIDTH = HEADS_PER_GROUP * HEAD_DIM
ATTN_QKV_WIDTH = N_GROUPS * ATTN_WIDTH
BAND = 128
SGU_WIDTH = D_MODEL // 2
SGU_GROUP_CH = 128
SGU_GROUPS = SGU_WIDTH // SGU_GROUP_CH
CHUNK = 128
IN_WIDTH = 3 * ATTN_QKV_WIDTH + 2 * SGU_WIDTH
ROPE_THETA = 500000.0
ROT_DIM = HEAD_DIM // 4
ROT_HALF = ROT_DIM // 2
XA_HEADS = 4
XA_WIDTH = XA_HEADS * HEAD_DIM
D_FF = 4 * D_MODEL
EPS = 1e-6
NEG = -1e30
ATTN_SCALE = HEAD_DIM ** -0.5
LANES = 128
BF16_SUBLANES = 16

VMEM_LIMIT = 56 * 1024 * 1024


def _params(sem, vmem=VMEM_LIMIT):
    return pltpu.CompilerParams(dimension_semantics=sem, vmem_limit_bytes=vmem)


def _gelu_tanh(x):
    c = math.sqrt(2.0 / math.pi)
    return 0.5 * x * (1.0 + jnp.tanh(c * (x + 0.044715 * (x * x * x))))


def _rms(v, g):
    return v * lax.rsqrt(jnp.mean(v * v, axis=-1, keepdims=True) + EPS) * g


def _cast_spec(w, grid):
    rows, cols = w.shape
    nblk = grid[0] * grid[1]
    while rows % nblk or (rows // nblk) % BF16_SUBLANES:
        nblk -= 1

    def imap(i, j):
        return jnp.minimum(i * grid[1] + j, nblk - 1), 0

    return pl.BlockSpec((rows // nblk, cols), imap)


def _call_with_casts(body, cast_ws, *, grid, in_specs, out_specs, out_shape, args, **kw):
    n_in, n_out, n_cast = len(in_specs), len(out_specs), len(cast_ws)

    def kern(*refs):
        ins, refs = refs[:n_in], refs[n_in:]
        cast_in, refs = refs[:n_cast], refs[n_cast:]
        outs, refs = refs[:n_out], refs[n_out:]
        cast_out, scratch = refs[:n_cast], refs[n_cast:]
        for src, dst in zip(cast_in, cast_out):
            dst[...] = src[...].astype(dst.dtype)
        body(*ins, *outs, *scratch)

    cspecs = [_cast_spec(w, grid) for w in cast_ws]
    return pl.pallas_call(
        kern,
        grid=grid,
        in_specs=list(in_specs) + cspecs,
        out_specs=list(out_specs) + cspecs,
        out_shape=list(out_shape) + [jax.ShapeDtypeStruct(w.shape, BF16) for w in cast_ws],
        **kw,
    )(*args, *cast_ws)


def _rope_table_kernel(pos_ref, inv_ref, c_ref, sa_ref, sb_ref):
    ang = pos_ref[...] * inv_ref[...]
    c = jnp.cos(ang)
    s = jnp.sin(ang)
    lane = lax.broadcasted_iota(jnp.int32, ang.shape, 1)
    c_ref[...] = jnp.where(lane < ROT_DIM, c, 1.0)
    sa_ref[...] = jnp.where(lane < ROT_HALF, -s, 0.0)
    sb_ref[...] = jnp.where((lane >= ROT_HALF) & (lane < ROT_DIM), s, 0.0)


def _rope_tables(positions):
    m = positions.size
    tm = 2048
    inv = ROPE_THETA ** (-jnp.arange(0, ROT_DIM, 2, dtype=F32) / ROT_DIM)
    inv_lane = jnp.concatenate([inv, inv, jnp.zeros((LANES - ROT_DIM,), F32)])[None, :]
    pos = positions.astype(F32).reshape(m, 1)
    out = jax.ShapeDtypeStruct((m, LANES), F32)
    spec = pl.BlockSpec((tm, LANES), lambda i: (i, 0))
    return pl.pallas_call(
        _rope_table_kernel,
        grid=(m // tm,),
        in_specs=[pl.BlockSpec((tm, 1), lambda i: (i, 0)), pl.BlockSpec((1, LANES), lambda i: (0, 0))],
        out_specs=[spec, spec, spec],
        out_shape=[out, out, out],
        compiler_params=_params(("parallel",)),
        name="rope_tables",
    )(pos, inv_lane)


def _rmsnorm_kernel(x_ref, g_ref, o_ref):
    o_ref[...] = _rms(x_ref[...], g_ref[...]).astype(o_ref.dtype)


def _rmsnorm(x2d, g, out_dtype=BF16, tm=512):
    m, d = x2d.shape
    return pl.pallas_call(
        _rmsnorm_kernel,
        grid=(m // tm,),
        in_specs=[pl.BlockSpec((tm, d), lambda i: (i, 0)), pl.BlockSpec((1, d), lambda i: (0, 0))],
        out_specs=pl.BlockSpec((tm, d), lambda i: (i, 0)),
        out_shape=jax.ShapeDtypeStruct((m, d), out_dtype),
        compiler_params=_params(("parallel",)),
        name="rmsnorm",
    )(x2d, g.reshape(1, d))


_UV_TILES = 2 * SGU_WIDTH // ATTN_WIDTH
_DOT_CHUNK = 2 * HEAD_DIM


def _inproj_qkv_kernel(h_ref, w_ref, c_ref, sa_ref, sb_ref, o_ref, *scratch, dil):
    kind = pl.program_id(1)
    tm = h_ref.shape[0]
    c = c_ref[...]
    sa = sa_ref[...]
    sb = sb_ref[...]
    rotate = kind < 2
    scale = jnp.where(kind == 0, ATTN_SCALE, 1.0).astype(F32)
    h = h_ref[...]
    for cc in range(ATTN_WIDTH // _DOT_CHUNK):
        acc = jnp.dot(h, w_ref[:, cc * _DOT_CHUNK:(cc + 1) * _DOT_CHUNK], preferred_element_type=F32)
        for hc in range(_DOT_CHUNK // HEAD_DIM):
            hh = cc * (_DOT_CHUNK // HEAD_DIM) + hc
            xh = acc[:, hc * HEAD_DIM:(hc + 1) * HEAD_DIM]
            y = xh * c + pltpu.roll(xh, LANES - ROT_HALF, 1) * sa + pltpu.roll(xh, ROT_HALF, 1) * sb
            y = jnp.where(rotate, y * scale, xh)
            if dil == 1:
                o_ref[0, :, hh * HEAD_DIM:(hh + 1) * HEAD_DIM] = y.astype(o_ref.dtype)
            else:
                scratch[0][hh] = y
    if dil > 1:
        for hh in range(HEADS_PER_GROUP):
            for r in range(dil):
                rows = scratch[0][hh, pl.ds(r, tm // dil, stride=dil), :]
                o_ref[r, :, hh * HEAD_DIM:(hh + 1) * HEAD_DIM] = rows.astype(o_ref.dtype)


def _inproj_uv_kernel(h_ref, w_ref, o_ref):
    h = h_ref[...]
    for cc in range(ATTN_WIDTH // _DOT_CHUNK):
        sl = slice(cc * _DOT_CHUNK, (cc + 1) * _DOT_CHUNK)
        acc = jnp.dot(h, w_ref[:, sl], preferred_element_type=F32)
        o_ref[:, sl] = _gelu_tanh(acc).astype(o_ref.dtype)


def _inproj_qkv(h, w_in_bf16, c_tab, sa_tab, sb_tab, g, batch, seq, tm=1024):
    m, k = h.shape
    tn = ATTN_WIDTH
    dil = ATTN_GROUPS[g][1]
    per_b = seq // tm
    tab_spec = pl.BlockSpec((tm, LANES), lambda i, kind: (i, 0))
    return pl.pallas_call(
        functools.partial(_inproj_qkv_kernel, dil=dil),
        grid=(m // tm, 3),
        in_specs=[pl.BlockSpec((tm, k), lambda i, kind: (i, 0)),
                  pl.BlockSpec((k, tn), lambda i, kind: (0, kind * N_GROUPS + g)),
                  tab_spec, tab_spec, tab_spec],
        out_specs=pl.BlockSpec((None, dil, tm // dil, tn), lambda i, kind: (i // per_b, 0, i % per_b, kind)),
        out_shape=jax.ShapeDtypeStruct((batch, dil, seq // dil, 3 * tn), BF16),
        scratch_shapes=[pltpu.VMEM((HEADS_PER_GROUP, tm, HEAD_DIM), F32)] if dil > 1 else [],
        compiler_params=_params(("parallel", "arbitrary")),
        name=f"inproj_qkv_g{g}",
    )(h, w_in_bf16, c_tab, sa_tab, sb_tab)


def _inproj_uv(h, w_in_bf16, cast_ws, tm=1024):
    m, k = h.shape
    tn = ATTN_WIDTH
    first = 3 * ATTN_QKV_WIDTH // tn
    return _call_with_casts(
        _inproj_uv_kernel, cast_ws,
        grid=(m // tm, _UV_TILES),
        in_specs=[pl.BlockSpec((tm, k), lambda i, j: (i, 0)),
                  pl.BlockSpec((k, tn), lambda i, j: (0, first + j))],
        out_specs=[pl.BlockSpec((tm, tn), lambda i, j: (i, j))],
        out_shape=[jax.ShapeDtypeStruct((m, _UV_TILES * tn), BF16)],
        args=(h, w_in_bf16),
        compiler_params=_params(("parallel", "arbitrary")),
        name="inproj_uv",
    )


def _band_attn_kernel(q_ref, kp_ref, kc_ref, vp_ref, vc_ref, o_ref, lse_ref, *scratch, dil):
    i = pl.program_id(1)
    r = pl.program_id(2)
    row = lax.broadcasted_iota(jnp.int32, (BAND, BAND), 0)
    col = lax.broadcasted_iota(jnp.int32, (BAND, BAND), 1)
    mask_cur = col <= row
    mask_prev = (col >= row) & (i > 0)

    def heads(ref):
        return jnp.stack([ref[:, hh * HEAD_DIM:(hh + 1) * HEAD_DIM] for hh in range(HEADS_PER_GROUP)])

    q, k_c, k_p, v_c, v_p = heads(q_ref), heads(kc_ref), heads(kp_ref), heads(vc_ref), heads(vp_ref)
    s_c = jnp.einsum("hqd,hkd->hqk", q, k_c, preferred_element_type=F32)
    s_p = jnp.einsum("hqd,hkd->hqk", q, k_p, preferred_element_type=F32)
    s_c = jnp.where(mask_cur, s_c, NEG)
    s_p = jnp.where(mask_prev, s_p, NEG)
    mx = jnp.maximum(jnp.max(s_c, axis=-1, keepdims=True), jnp.max(s_p, axis=-1, keepdims=True))
    p_c = jnp.exp(s_c - mx)
    p_p = jnp.exp(s_p - mx)
    den = jnp.sum(p_c, axis=-1, keepdims=True) + jnp.sum(p_p, axis=-1, keepdims=True)
    o = jnp.einsum("hqk,hkd->hqd", p_c.astype(BF16), v_c, preferred_element_type=F32)
    o = o + jnp.einsum("hqk,hkd->hqd", p_p.astype(BF16), v_p, preferred_element_type=F32)
    o = o / den
    lse = mx + jnp.log(den)
    lse_tile = jnp.zeros((BAND, LANES), F32)
    for hh in range(HEADS_PER_GROUP):
        if dil == 1:
            o_ref[:, hh * HEAD_DIM:(hh + 1) * HEAD_DIM] = o[hh].astype(o_ref.dtype)
        else:
            scratch[0][hh, pl.ds(r, BAND, stride=dil), :] = o[hh]
        lse_tile = jnp.where(col == hh, lse[hh], lse_tile)
    if dil == 1:
        lse_ref[...] = lse_tile
    else:
        lse_ref[pl.ds(r, BAND, stride=dil), :] = lse_tile

        @pl.when(r == dil - 1)
        def _():
            for hh in range(HEADS_PER_GROUP):
                o_ref[:, hh * HEAD_DIM:(hh + 1) * HEAD_DIM] = scratch[0][hh].astype(o_ref.dtype)


def _band_attention(qkv, g, batch, seq):
    _, dil = ATTN_GROUPS[g]
    sub = seq // dil
    nblk = sub // BAND
    rows = BAND * dil

    def spec(col, prev):
        if prev:
            return pl.BlockSpec((None, None, BAND, ATTN_WIDTH), lambda b, i, r: (b, r, jnp.maximum(i - 1, 0), col))
        return pl.BlockSpec((None, None, BAND, ATTN_WIDTH), lambda b, i, r: (b, r, i, col))

    return pl.pallas_call(
        functools.partial(_band_attn_kernel, dil=dil),
        grid=(batch, nblk, dil),
        in_specs=[spec(0, False), spec(1, True), spec(1, False), spec(2, True), spec(2, False)],
        out_specs=[pl.BlockSpec((rows, ATTN_WIDTH), lambda b, i, r: (b * nblk + i, 0)),
                   pl.BlockSpec((rows, LANES), lambda b, i, r: (b * nblk + i, 0))],
        out_shape=[jax.ShapeDtypeStruct((batch * seq, ATTN_WIDTH), BF16),
                   jax.ShapeDtypeStruct((batch * seq, LANES), F32)],
        scratch_shapes=[pltpu.VMEM((HEADS_PER_GROUP, rows, HEAD_DIM), F32)] if dil > 1 else [],
        compiler_params=_params(("parallel", "parallel", "arbitrary")),
        name=f"band_attn_g{g}",
    )(qkv, qkv, qkv, qkv, qkv)


def _combine_kernel(o0_ref, o1_ref, o2_ref, l0_ref, l1_ref, l2_ref, y_ref):
    l0, l1, l2 = l0_ref[...], l1_ref[...], l2_ref[...]
    mx = jnp.maximum(jnp.maximum(l0, l1), l2)
    e0, e1, e2 = jnp.exp(l0 - mx), jnp.exp(l1 - mx), jnp.exp(l2 - mx)
    den = e0 + e1 + e2
    a0, a1, a2 = e0 / den, e1 / den, e2 / den
    rows = l0.shape[0]
    for hh in range(HEADS_PER_GROUP):
        sl = slice(hh * HEAD_DIM, (hh + 1) * HEAD_DIM)
        b0 = jnp.broadcast_to(a0[:, hh:hh + 1], (rows, HEAD_DIM))
        b1 = jnp.broadcast_to(a1[:, hh:hh + 1], (rows, HEAD_DIM))
        b2 = jnp.broadcast_to(a2[:, hh:hh + 1], (rows, HEAD_DIM))
        y = b0 * o0_ref[:, sl].astype(F32) + b1 * o1_ref[:, sl].astype(F32) + b2 * o2_ref[:, sl].astype(F32)
        y_ref[:, sl] = y.astype(y_ref.dtype)


def _combine(os, lses, tm=512):
    m = os[0].shape[0]
    ospec = pl.BlockSpec((tm, ATTN_WIDTH), lambda i: (i, 0))
    lspec = pl.BlockSpec((tm, LANES), lambda i: (i, 0))
    return pl.pallas_call(
        _combine_kernel,
        grid=(m // tm,),
        in_specs=[ospec, ospec, ospec, lspec, lspec, lspec],
        out_specs=ospec,
        out_shape=jax.ShapeDtypeStruct((m, ATTN_WIDTH), BF16),
        compiler_params=_params(("parallel",)),
        name="attn_combine",
    )(*os, *lses)


_SGU_ROWS = 4 * CHUNK


def _sgu_kernel(u0_ref, u1_ref, v0_ref, v1_ref, g_ref, b_ref, ws_ref, bt_ref, o_ref):
    half = SGU_WIDTH // 2
    nch = v0_ref.shape[0] // CHUNK
    v0 = v0_ref[...].astype(F32)
    v1 = v1_ref[...].astype(F32)
    mu = (jnp.sum(v0, axis=-1, keepdims=True) + jnp.sum(v1, axis=-1, keepdims=True)) / SGU_WIDTH
    d0, d1 = v0 - mu, v1 - mu
    var = (jnp.sum(d0 * d0, axis=-1, keepdims=True) + jnp.sum(d1 * d1, axis=-1, keepdims=True)) / SGU_WIDTH
    inv = lax.rsqrt(var + EPS)
    vn = ((d0 * inv * g_ref[:, :half] + b_ref[:, :half]).astype(BF16),
          (d1 * inv * g_ref[:, half:] + b_ref[:, half:]).astype(BF16))
    u_refs = (u0_ref, u1_ref)
    row = lax.broadcasted_iota(jnp.int32, (CHUNK, CHUNK), 0)
    col = lax.broadcasted_iota(jnp.int32, (CHUNK, CHUNK), 1)
    tri = col <= row
    per_half = SGU_GROUPS // 2
    for gg in range(SGU_GROUPS):
        hf, gl = divmod(gg, per_half)
        sl = slice(gl * SGU_GROUP_CH, (gl + 1) * SGU_GROUP_CH)
        osl = slice(gg * SGU_GROUP_CH, (gg + 1) * SGU_GROUP_CH)
        w = jnp.where(tri, ws_ref[gg], 0.0).astype(BF16)
        v_all = jnp.concatenate([vn[hf][c * CHUNK:(c + 1) * CHUNK, sl] for c in range(nch)], axis=1)
        mixed = jnp.dot(w, v_all, preferred_element_type=F32)
        bias = bt_ref[:, gg:gg + 1]
        for c in range(nch):
            rs = slice(c * CHUNK, (c + 1) * CHUNK)
            m_c = mixed[:, c * SGU_GROUP_CH:(c + 1) * SGU_GROUP_CH] + bias
            o_ref[rs, osl] = (u_refs[hf][rs, sl].astype(F32) * m_c).astype(o_ref.dtype)


def _sgu(proj, ln_g, ln_b, w_spatial, b_spatial):
    m = proj.shape[0]
    half = SGU_WIDTH // 2
    vec = pl.BlockSpec((1, SGU_WIDTH), lambda i: (0, 0))

    def col_spec(c):
        return pl.BlockSpec((_SGU_ROWS, half), lambda i: (i, c))

    return pl.pallas_call(
        _sgu_kernel,
        grid=(m // _SGU_ROWS,),
        in_specs=[col_spec(0), col_spec(1), col_spec(2), col_spec(3),
                  vec, vec,
                  pl.BlockSpec((SGU_GROUPS, CHUNK, CHUNK), lambda i: (0, 0, 0)),
                  pl.BlockSpec((CHUNK, SGU_GROUPS), lambda i: (0, 0))],
        out_specs=pl.BlockSpec((_SGU_ROWS, SGU_WIDTH), lambda i: (i, 0)),
        out_shape=jax.ShapeDtypeStruct((m, SGU_WIDTH), BF16),
        compiler_params=_params(("parallel",)),
        name="sgu",
    )(proj, proj, proj, proj, ln_g.reshape(1, -1), ln_b.reshape(1, -1), w_spatial, b_spatial.T)


def _merge_kernel(h_ref, ya_ref, yb_ref, wga_ref, wgb_ref, wa_ref, wb_ref, bga_ref, bgb_ref, o_ref):
    h = h_ref[...]
    ga = jax.nn.sigmoid(jnp.dot(h, wga_ref[...], preferred_element_type=F32) + bga_ref[...])
    a = jnp.dot(ya_ref[...], wa_ref[...], preferred_element_type=F32)
    acc = ga * a
    gb = jax.nn.sigmoid(jnp.dot(h, wgb_ref[...], preferred_element_type=F32) + bgb_ref[...])
    b = jnp.dot(yb_ref[...], wb_ref[...], preferred_element_type=F32)
    o_ref[...] = (acc + gb * b).astype(o_ref.dtype)


def _merge(h, ya, yb, w_gate, b_gate, w_a, w_b, cast_ws, tm=512, tn=512):
    m = h.shape[0]
    nb = D_MODEL // tn
    b_gate = b_gate.reshape(1, -1)
    return _call_with_casts(
        _merge_kernel, cast_ws,
        grid=(m // tm, nb),
        in_specs=[pl.BlockSpec((tm, D_MODEL), lambda i, j: (i, 0)),
                  pl.BlockSpec((tm, ATTN_WIDTH), lambda i, j: (i, 0)),
                  pl.BlockSpec((tm, SGU_WIDTH), lambda i, j: (i, 0)),
                  pl.BlockSpec((D_MODEL, tn), lambda i, j: (0, j)),
                  pl.BlockSpec((D_MODEL, tn), lambda i, j: (0, j + nb)),
                  pl.BlockSpec((ATTN_WIDTH, tn), lambda i, j: (0, j)),
                  pl.BlockSpec((SGU_WIDTH, tn), lambda i, j: (0, j)),
                  pl.BlockSpec((1, tn), lambda i, j: (0, j)),
                  pl.BlockSpec((1, tn), lambda i, j: (0, j + nb))],
        out_specs=[pl.BlockSpec((tm, tn), lambda i, j: (i, j))],
        out_shape=[jax.ShapeDtypeStruct((m, D_MODEL), BF16)],
        args=(h, ya, yb, w_gate, w_gate, w_a, w_b, b_gate, b_gate),
        compiler_params=_params(("parallel", "arbitrary")),
        name="gated_merge",
    )


def _matmul_kernel(a_ref, w_ref, o_ref, *, act):
    acc = jnp.dot(a_ref[...], w_ref[...], preferred_element_type=F32)
    if act == "relu2":
        r = jnp.maximum(acc, 0.0)
        acc = r * r
    o_ref[...] = acc.astype(o_ref.dtype)


def _matmul(a, w, out_dtype, act=None, tm=1024, tn=1024, name="matmul", cast_ws=()):
    m, k = a.shape
    n = w.shape[1]
    tm, tn = min(tm, m), min(tn, n)
    return _call_with_casts(
        functools.partial(_matmul_kernel, act=act), cast_ws,
        grid=(m // tm, n // tn),
        in_specs=[pl.BlockSpec((tm, k), lambda i, j: (i, 0)), pl.BlockSpec((k, tn), lambda i, j: (0, j))],
        out_specs=[pl.BlockSpec((tm, tn), lambda i, j: (i, j))],
        out_shape=[jax.ShapeDtypeStruct((m, n), out_dtype)],
        args=(a, w),
        compiler_params=_params(("parallel", "arbitrary")),
        name=name,
    )


_NORM_DOT_CHUNK = 1024


def _matmul_norm_kernel(a_ref, w_ref, x_ref, g_ref, xo_ref, xs_ref):
    k = pl.program_id(1)
    d = xo_ref.shape[1]
    xw = x_ref.shape[1]
    xs_ref[:, pl.ds(pl.multiple_of(k * xw, xw), xw)] = x_ref[...]
    a = a_ref[...]
    for c in range(d // _NORM_DOT_CHUNK):
        sl = slice(c * _NORM_DOT_CHUNK, (c + 1) * _NORM_DOT_CHUNK)
        part = jnp.dot(a, w_ref[:, sl], preferred_element_type=F32)
        xo_ref[:, sl] = jnp.where(k == 0, 0.0, xo_ref[:, sl]) + part

    @pl.when(k == pl.num_programs(1) - 1)
    def _():
        xo_ref[...] = xs_ref[...] + _rms(xo_ref[...], g_ref[...])


def _matmul_norm_residual(a, w, x, g, name, tm=512, tk=1024):
    m, kdim = a.shape
    d = w.shape[1]
    nk = kdim // tk
    return pl.pallas_call(
        _matmul_norm_kernel,
        grid=(m // tm, nk),
        in_specs=[pl.BlockSpec((tm, tk), lambda i, k: (i, k)),
                  pl.BlockSpec((tk, d), lambda i, k: (k, 0)),
                  pl.BlockSpec((tm, d // nk), lambda i, k: (i, k)),
                  pl.BlockSpec((1, d), lambda i, k: (0, 0))],
        out_specs=pl.BlockSpec((tm, d), lambda i, k: (i, 0)),
        out_shape=jax.ShapeDtypeStruct((m, d), F32),
        scratch_shapes=[pltpu.VMEM((tm, d), F32)],
        compiler_params=_params(("parallel", "arbitrary")),
        name=name,
    )(a, w, x, g.reshape(1, d))


def _xattn_kernel(x_ref, gpre_ref, wq_ref, k_ref, v_ref, wo_ref, gpost_ref, gnext_ref, xo_ref, hn_ref):
    x = x_ref[...]
    h = _rms(x, gpre_ref[...]).astype(BF16)
    q = jnp.dot(h, wq_ref[...], preferred_element_type=F32) * ATTN_SCALE
    q = q.astype(BF16)
    dn = (((1,), (1,)), ((), ()))
    outs = []
    for hh in range(XA_HEADS):
        sl = slice(hh * HEAD_DIM, (hh + 1) * HEAD_DIM)
        s = lax.dot_general(q[:, sl], k_ref[:, sl], dn, preferred_element_type=F32)
        mx = jnp.max(s, axis=-1, keepdims=True)
        p = jnp.exp(s - mx)
        den = jnp.sum(p, axis=-1, keepdims=True)
        o = jnp.dot(p.astype(BF16), v_ref[:, sl], preferred_element_type=F32)
        outs.append((o / den).astype(BF16))
    o_all = jnp.concatenate(outs, axis=-1)
    y = jnp.dot(o_all, wo_ref[...], preferred_element_type=F32)
    x_new = x + _rms(y, gpost_ref[...])
    xo_ref[...] = x_new
    hn_ref[...] = _rms(x_new, gnext_ref[...]).astype(hn_ref.dtype)


def _xattn(x, g_pre, w_xq, k_mem, v_mem, w_xo, g_post, g_next, batch, seq, tm=256):
    m, d = x.shape
    n_mem = k_mem.shape[0] // batch
    per_b = seq // tm
    row = pl.BlockSpec((tm, d), lambda i: (i, 0))
    vec = pl.BlockSpec((1, d), lambda i: (0, 0))
    return pl.pallas_call(
        _xattn_kernel,
        grid=(m // tm,),
        in_specs=[row, vec,
                  pl.BlockSpec((d, XA_WIDTH), lambda i: (0, 0)),
                  pl.BlockSpec((n_mem, XA_WIDTH), lambda i: (i // per_b, 0)),
                  pl.BlockSpec((n_mem, XA_WIDTH), lambda i: (i // per_b, 0)),
                  pl.BlockSpec((XA_WIDTH, d), lambda i: (0, 0)),
                  vec, vec],
        out_specs=[row, row],
        out_shape=[jax.ShapeDtypeStruct((m, d), F32), jax.ShapeDtypeStruct((m, d), BF16)],
        compiler_params=_params(("parallel",)),
        name="cross_attn",
    )(x, g_pre.reshape(1, d), w_xq, k_mem, v_mem, w_xo, g_post.reshape(1, d), g_next.reshape(1, d))


def kernel(x, mem, positions, mix_pre_g, w_in, sgu_ln_g, sgu_ln_b, w_spatial, b_spatial, w_branch_a, w_branch_b, w_gate, b_gate, w_out, mix_post_g, xa_pre_g, mem_norm_g, w_xq, w_xk, w_xv, w_xo, xa_post_g, mlp_pre_g, w_up, w_down, mlp_post_g):
    batch, seq, d = x.shape
    n_mem = mem.shape[1]
    depth = w_in.shape[0]
    m = batch * seq
    x2 = x.reshape(m, d)
    mem2 = mem.reshape(batch * n_mem, d)
    c_tab, sa_tab, sb_tab = _rope_tables(positions)

    for l in range(depth):
        w_in_bf = w_in[l].astype(BF16)

        h = _rmsnorm(x2, mix_pre_g[l])
        uv, w_gate_bf, w_a_bf, w_b_bf = _inproj_uv(h, w_in_bf, (w_gate[l], w_branch_a[l], w_branch_b[l]))
        os, lses = [], []
        for g in range(N_GROUPS):
            qkv = _inproj_qkv(h, w_in_bf, c_tab, sa_tab, sb_tab, g, batch, seq)
            o, lse = _band_attention(qkv, g, batch, seq)
            os.append(o)
            lses.append(lse)
        y_a = _combine(os, lses)
        y_b = _sgu(uv, sgu_ln_g[l], sgu_ln_b[l], w_spatial[l], b_spatial[l])
        merged, w_up_bf, w_out_bf, w_xq_bf, w_xk_bf, w_xv_bf, w_xo_bf = _merge(
            h, y_a, y_b, w_gate_bf, b_gate[l], w_a_bf, w_b_bf,
            (w_up[l], w_out[l], w_xq[l], w_xk[l], w_xv[l], w_xo[l]))
        x2 = _matmul_norm_residual(merged, w_out_bf, x2, mix_post_g[l], "out_proj_norm")

        mn = _rmsnorm(mem2, mem_norm_g[l])
        k_mem, = _matmul(mn, w_xk_bf, BF16, name="mem_k")
        v_mem, = _matmul(mn, w_xv_bf, BF16, name="mem_v")
        x2, h = _xattn(x2, xa_pre_g[l], w_xq_bf, k_mem, v_mem, w_xo_bf, xa_post_g[l], mlp_pre_g[l], batch, seq)

        a, w_down_bf = _matmul(h, w_up_bf, BF16, act="relu2", name="mlp_up", cast_ws=(w_down[l],))
        x2 = _matmul_norm_residual(a, w_down_bf, x2, mlp_post_g[l], "mlp_down_norm")
    return x2.reshape(batch, seq, d)
```

```python
import functools
import math

import jax
import jax.numpy as jnp
from jax import lax
from jax.experimental import pallas as pl
from jax.experimental.pallas import tpu as pltpu

F32 = jnp.float32
BF16 = jnp.bfloat16

D_MODEL = 4096
HEAD_DIM = 128
ATTN_GROUPS = ((128, 1), (512, 4), (2048, 16))
N_GROUPS = len(ATTN_GROUPS)
HEADS_PER_GROUP = 8
ATTN_WIDTH = HEADS_PER_GROUP * HEAD_DIM
ATTN_QKV_WIDTH = N_GROUPS * ATTN_WIDTH
BAND = 128
SGU_WIDTH = D_MODEL // 2
SGU_GROUP_CH = 128
SGU_GROUPS = SGU_WIDTH // SGU_GROUP_CH
CHUNK = 128
IN_WIDTH = 3 * ATTN_QKV_WIDTH + 2 * SGU_WIDTH
ROPE_THETA = 500000.0
ROT_DIM = HEAD_DIM // 4
ROT_HALF = ROT_DIM // 2
XA_HEADS = 4
XA_WIDTH = XA_HEADS * HEAD_DIM
D_FF = 4 * D_MODEL
EPS = 1e-6
NEG = -1e30
ATTN_SCALE = HEAD_DIM ** -0.5
LANES = 128
BF16_SUBLANES = 16

VMEM_LIMIT = 56 * 1024 * 1024


def _params(sem, vmem=VMEM_LIMIT):
    return pltpu.CompilerParams(dimension_semantics=sem, vmem_limit_bytes=vmem)


def _gelu_tanh(x):
    c = math.sqrt(2.0 / math.pi)
    return 0.5 * x * (1.0 + jnp.tanh(c * (x + 0.044715 * (x * x * x))))


def _rms(v, g):
    return v * lax.rsqrt(jnp.mean(v * v, axis=-1, keepdims=True) + EPS) * g


def _cast_spec(w, grid):
    rows, cols = w.shape
    nblk = math.prod(grid)
    while rows % nblk or (rows // nblk) % BF16_SUBLANES:
        nblk -= 1

    def imap(*idx):
        step = idx[0]
        for extent, ix in zip(grid[1:], idx[1:]):
            step = step * extent + ix
        return jnp.minimum(step, nblk - 1), 0

    return pl.BlockSpec((rows // nblk, cols), imap)


def _call_with_casts(body, cast_ws, *, grid, in_specs, out_specs, out_shape, args, **kw):
    n_in, n_out, n_cast = len(in_specs), len(out_specs), len(cast_ws)

    def kern(*refs):
        ins, refs = refs[:n_in], refs[n_in:]
        cast_in, refs = refs[:n_cast], refs[n_cast:]
        outs, refs = refs[:n_out], refs[n_out:]
        cast_out, scratch = refs[:n_cast], refs[n_cast:]
        for src, dst in zip(cast_in, cast_out):
            dst[...] = src[...].astype(dst.dtype)
        body(*ins, *outs, *scratch)

    cspecs = [_cast_spec(w, grid) for w in cast_ws]
    return pl.pallas_call(
        kern,
        grid=grid,
        in_specs=list(in_specs) + cspecs,
        out_specs=list(out_specs) + cspecs,
        out_shape=list(out_shape) + [jax.ShapeDtypeStruct(w.shape, BF16) for w in cast_ws],
        **kw,
    )(*args, *cast_ws)


def _rope_table_kernel(pos_ref, inv_ref, c_ref, sa_ref, sb_ref):
    ang = pos_ref[...] * inv_ref[...]
    c = jnp.cos(ang)
    s = jnp.sin(ang)
    lane = lax.broadcasted_iota(jnp.int32, ang.shape, 1)
    c_ref[...] = jnp.where(lane < ROT_DIM, c, 1.0)
    sa_ref[...] = jnp.where(lane < ROT_HALF, -s, 0.0)
    sb_ref[...] = jnp.where((lane >= ROT_HALF) & (lane < ROT_DIM), s, 0.0)


def _rope_tables(positions):
    m = positions.size
    tm = 2048
    inv = ROPE_THETA ** (-jnp.arange(0, ROT_DIM, 2, dtype=F32) / ROT_DIM)
    inv_lane = jnp.concatenate([inv, inv, jnp.zeros((LANES - ROT_DIM,), F32)])[None, :]
    pos = positions.astype(F32).reshape(m, 1)
    out = jax.ShapeDtypeStruct((m, LANES), F32)
    spec = pl.BlockSpec((tm, LANES), lambda i: (i, 0))
    return pl.pallas_call(
        _rope_table_kernel,
        grid=(m // tm,),
        in_specs=[pl.BlockSpec((tm, 1), lambda i: (i, 0)), pl.BlockSpec((1, LANES), lambda i: (0, 0))],
        out_specs=[spec, spec, spec],
        out_shape=[out, out, out],
        compiler_params=_params(("parallel",)),
        name="rope_tables",
    )(pos, inv_lane)


def _rmsnorm_kernel(x_ref, g_ref, o_ref):
    o_ref[...] = _rms(x_ref[...], g_ref[...]).astype(o_ref.dtype)


def _rmsnorm(x2d, g, out_dtype=BF16, tm=512):
    m, d = x2d.shape
    return pl.pallas_call(
        _rmsnorm_kernel,
        grid=(m // tm,),
        in_specs=[pl.BlockSpec((tm, d), lambda i: (i, 0)), pl.BlockSpec((1, d), lambda i: (0, 0))],
        out_specs=pl.BlockSpec((tm, d), lambda i: (i, 0)),
        out_shape=jax.ShapeDtypeStruct((m, d), out_dtype),
        compiler_params=_params(("parallel",)),
        name="rmsnorm",
    )(x2d, g.reshape(1, d))


_UV_TILES = 2 * SGU_WIDTH // ATTN_WIDTH
_MAX_FAST_STRIDE = 4
_DOT_CHUNK = 2 * HEAD_DIM


def _pipelined_tiles(dot_into, finish, acc_a, acc_b):
    s = pl.program_id(0)
    last = pl.num_programs(0) - 1

    @pl.when(s == 0)
    def _():
        acc_b[...] = jnp.zeros_like(acc_b)

    @pl.when((s % 2 == 0) & (s < last))
    def _():
        dot_into(acc_a)
        finish(acc_b)

    @pl.when(s % 2 == 1)
    def _():
        dot_into(acc_b)
        finish(acc_a)

    @pl.when(s == last)
    def _():
        finish(acc_b)


def _dot_into_heads(h_ref, w_ref, acc):
    h = h_ref[...]
    for cc in range(ATTN_WIDTH // _DOT_CHUNK):
        part = jnp.dot(h, w_ref[:, cc * _DOT_CHUNK:(cc + 1) * _DOT_CHUNK], preferred_element_type=F32)
        for hc in range(_DOT_CHUNK // HEAD_DIM):
            acc[cc * (_DOT_CHUNK // HEAD_DIM) + hc] = part[:, hc * HEAD_DIM:(hc + 1) * HEAD_DIM]


def _inproj_qkv_kernel(h_ref, w_ref, c_ref, sa_ref, sb_ref, o_ref, acc_a, acc_b, *scratch, dil):
    tm = h_ref.shape[0]

    def finish(acc):
        kind = jnp.maximum(pl.program_id(0) - 1, 0) % 3
        c = c_ref[...]
        sa = sa_ref[...]
        sb = sb_ref[...]
        rotate = kind < 2
        scale = jnp.where(kind == 0, ATTN_SCALE, 1.0).astype(F32)
        for hh in range(HEADS_PER_GROUP):
            osl = slice(hh * HEAD_DIM, (hh + 1) * HEAD_DIM)
            xh = acc[hh]
            y = xh * c + pltpu.roll(xh, LANES - ROT_HALF, 1) * sa + pltpu.roll(xh, ROT_HALF, 1) * sb
            y = jnp.where(rotate, y * scale, xh)
            if dil == 1:
                o_ref[0, :, osl] = y.astype(o_ref.dtype)
                continue
            acc[hh] = y
            if dil <= _MAX_FAST_STRIDE:
                for r in range(dil):
                    rows = acc[hh, pl.ds(r, tm // dil, stride=dil), :]
                    o_ref[r, :, osl] = rows.astype(o_ref.dtype)
            else:
                outer = dil // _MAX_FAST_STRIDE
                for r_lo in range(_MAX_FAST_STRIDE):
                    scratch[0][hh] = acc[hh, pl.ds(r_lo, tm // _MAX_FAST_STRIDE, stride=_MAX_FAST_STRIDE), :]
                    for r_hi in range(outer):
                        rows = scratch[0][hh, pl.ds(r_hi, tm // dil, stride=outer), :]
                        o_ref[r_hi * _MAX_FAST_STRIDE + r_lo, :, osl] = rows.astype(o_ref.dtype)

    _pipelined_tiles(functools.partial(_dot_into_heads, h_ref, w_ref), finish, acc_a, acc_b)


def _inproj_uv_kernel(h_ref, w_ref, o_ref):
    h = h_ref[...]
    for cc in range(ATTN_WIDTH // _DOT_CHUNK):
        sl = slice(cc * _DOT_CHUNK, (cc + 1) * _DOT_CHUNK)
        acc = jnp.dot(h, w_ref[:, sl], preferred_element_type=F32)
        o_ref[:, sl] = _gelu_tanh(acc).astype(o_ref.dtype)


def _tile_maps(n_tiles):
    assert n_tiles % 2 == 0
    return (lambda s: jnp.minimum(s, n_tiles - 1)), (lambda s: jnp.maximum(s - 1, 0))


def _inproj_qkv(h, w_in_bf16, c_tab, sa_tab, sb_tab, g, batch, seq, tm=1024):
    m, k = h.shape
    tn = ATTN_WIDTH
    dil = ATTN_GROUPS[g][1]
    per_b = seq // tm
    n_tiles = (m // tm) * 3
    dot_tile, fin_tile = _tile_maps(n_tiles)

    def out_map(s):
        i, kind = fin_tile(s) // 3, fin_tile(s) % 3
        return i // per_b, 0, i % per_b, kind

    tab_spec = pl.BlockSpec((tm, LANES), lambda s: (fin_tile(s) // 3, 0))
    acc = pltpu.VMEM((HEADS_PER_GROUP, tm, HEAD_DIM), F32)
    scratch = [acc, acc]
    if dil > _MAX_FAST_STRIDE:
        scratch.append(pltpu.VMEM((HEADS_PER_GROUP, tm // _MAX_FAST_STRIDE, HEAD_DIM), F32))
    return pl.pallas_call(
        functools.partial(_inproj_qkv_kernel, dil=dil),
        grid=(n_tiles + 1,),
        in_specs=[pl.BlockSpec((tm, k), lambda s: (dot_tile(s) // 3, 0)),
                  pl.BlockSpec((k, tn), lambda s: (0, (dot_tile(s) % 3) * N_GROUPS + g)),
                  tab_spec, tab_spec, tab_spec],
        out_specs=pl.BlockSpec((None, dil, tm // dil, tn), out_map),
        out_shape=jax.ShapeDtypeStruct((batch, dil, seq // dil, 3 * tn), BF16),
        scratch_shapes=scratch,
        compiler_params=_params(("arbitrary",)),
        name=f"inproj_qkv_g{g}",
    )(h, w_in_bf16, c_tab, sa_tab, sb_tab)


def _inproj_uv(h, w_in_bf16, cast_ws, tm=1024):
    m, k = h.shape
    tn = ATTN_WIDTH
    first = 3 * ATTN_QKV_WIDTH // tn
    return _call_with_casts(
        _inproj_uv_kernel, cast_ws,
        grid=(m // tm, _UV_TILES),
        in_specs=[pl.BlockSpec((tm, k), lambda i, j: (i, 0)),
                  pl.BlockSpec((k, tn), lambda i, j: (0, first + j))],
        out_specs=[pl.BlockSpec((tm, tn), lambda i, j: (i, j))],
        out_shape=[jax.ShapeDtypeStruct((m, _UV_TILES * tn), BF16)],
        args=(h, w_in_bf16),
        compiler_params=_params(("parallel", "arbitrary")),
        name="inproj_uv",
    )


_ATTN_QBLOCKS = 2


def _band_attn_kernel(q_ref, kp_ref, kc_ref, vp_ref, vc_ref, o_ref, lse_ref, *scratch, dil):
    i = pl.program_id(1)
    r = pl.program_id(2)
    row = lax.broadcasted_iota(jnp.int32, (BAND, BAND), 0)
    col = lax.broadcasted_iota(jnp.int32, (BAND, BAND), 1)
    mask_cur = col <= row
    mask_band = col >= row

    def heads(ref, blk):
        rows = slice(blk * BAND, (blk + 1) * BAND)
        return jnp.stack([ref[rows, hh * HEAD_DIM:(hh + 1) * HEAD_DIM] for hh in range(HEADS_PER_GROUP)])

    for blk in range(_ATTN_QBLOCKS):
        q, k_c, v_c = heads(q_ref, blk), heads(kc_ref, blk), heads(vc_ref, blk)
        if blk == 0:
            k_p, v_p = heads(kp_ref, 0), heads(vp_ref, 0)
            mask_prev = mask_band & (i > 0)
        else:
            k_p, v_p = heads(kc_ref, blk - 1), heads(vc_ref, blk - 1)
            mask_prev = mask_band
        s_c = jnp.einsum("hqd,hkd->hqk", q, k_c, preferred_element_type=F32)
        s_p = jnp.einsum("hqd,hkd->hqk", q, k_p, preferred_element_type=F32)
        s_c = jnp.where(mask_cur, s_c, NEG)
        s_p = jnp.where(mask_prev, s_p, NEG)
        mx = jnp.maximum(jnp.max(s_c, axis=-1, keepdims=True), jnp.max(s_p, axis=-1, keepdims=True))
        p_c = jnp.exp(s_c - mx)
        p_p = jnp.exp(s_p - mx)
        den = jnp.sum(p_c, axis=-1, keepdims=True) + jnp.sum(p_p, axis=-1, keepdims=True)
        o = jnp.einsum("hqk,hkd->hqd", p_c.astype(BF16), v_c, preferred_element_type=F32)
        o = o + jnp.einsum("hqk,hkd->hqd", p_p.astype(BF16), v_p, preferred_element_type=F32)
        o = o / den
        lse = mx + jnp.log(den)
        lse_tile = jnp.zeros((BAND, LANES), F32)
        rows = slice(blk * BAND, (blk + 1) * BAND) if dil == 1 else pl.ds(blk * BAND * dil + r, BAND, stride=dil)
        for hh in range(HEADS_PER_GROUP):
            if dil == 1:
                o_ref[rows, hh * HEAD_DIM:(hh + 1) * HEAD_DIM] = o[hh].astype(o_ref.dtype)
            else:
                scratch[0][hh, rows, :] = o[hh]
            lse_tile = jnp.where(col == hh, lse[hh], lse_tile)
        lse_ref[rows, :] = lse_tile

    if dil > 1:
        @pl.when(r == dil - 1)
        def _():
            for hh in range(HEADS_PER_GROUP):
                o_ref[:, hh * HEAD_DIM:(hh + 1) * HEAD_DIM] = scratch[0][hh].astype(o_ref.dtype)


def _band_attention(qkv, g, batch, seq):
    _, dil = ATTN_GROUPS[g]
    sub = seq // dil
    qrows = _ATTN_QBLOCKS * BAND
    nstep = sub // qrows
    rows = qrows * dil

    def cur(col):
        return pl.BlockSpec((None, None, qrows, ATTN_WIDTH), lambda b, i, r: (b, r, i, col))

    def prev(col):
        return pl.BlockSpec((None, None, BAND, ATTN_WIDTH),
                            lambda b, i, r: (b, r, jnp.maximum(i * _ATTN_QBLOCKS - 1, 0), col))

    return pl.pallas_call(
        functools.partial(_band_attn_kernel, dil=dil),
        grid=(batch, nstep, dil),
        in_specs=[cur(0), prev(1), cur(1), prev(2), cur(2)],
        out_specs=[pl.BlockSpec((rows, ATTN_WIDTH), lambda b, i, r: (b * nstep + i, 0)),
                   pl.BlockSpec((rows, LANES), lambda b, i, r: (b * nstep + i, 0))],
        out_shape=[jax.ShapeDtypeStruct((batch * seq, ATTN_WIDTH), BF16),
                   jax.ShapeDtypeStruct((batch * seq, LANES), F32)],
        scratch_shapes=[pltpu.VMEM((HEADS_PER_GROUP, rows, HEAD_DIM), F32)] if dil > 1 else [],
        compiler_params=_params(("parallel", "parallel", "arbitrary")),
        name=f"band_attn_g{g}",
    )(qkv, qkv, qkv, qkv, qkv)


def _combine_kernel(o0_ref, o1_ref, o2_ref, l0_ref, l1_ref, l2_ref, y_ref):
    l0, l1, l2 = l0_ref[...], l1_ref[...], l2_ref[...]
    mx = jnp.maximum(jnp.maximum(l0, l1), l2)
    e0, e1, e2 = jnp.exp(l0 - mx), jnp.exp(l1 - mx), jnp.exp(l2 - mx)
    den = e0 + e1 + e2
    a0, a1, a2 = e0 / den, e1 / den, e2 / den
    rows = l0.shape[0]
    for hh in range(HEADS_PER_GROUP):
        sl = slice(hh * HEAD_DIM, (hh + 1) * HEAD_DIM)
        b0 = jnp.broadcast_to(a0[:, hh:hh + 1], (rows, HEAD_DIM))
        b1 = jnp.broadcast_to(a1[:, hh:hh + 1], (rows, HEAD_DIM))
        b2 = jnp.broadcast_to(a2[:, hh:hh + 1], (rows, HEAD_DIM))
        y = b0 * o0_ref[:, sl].astype(F32) + b1 * o1_ref[:, sl].astype(F32) + b2 * o2_ref[:, sl].astype(F32)
        y_ref[:, sl] = y.astype(y_ref.dtype)


def _combine(os, lses, tm=512):
    m = os[0].shape[0]
    ospec = pl.BlockSpec((tm, ATTN_WIDTH), lambda i: (i, 0))
    lspec = pl.BlockSpec((tm, LANES), lambda i: (i, 0))
    return pl.pallas_call(
        _combine_kernel,
        grid=(m // tm,),
        in_specs=[ospec, ospec, ospec, lspec, lspec, lspec],
        out_specs=ospec,
        out_shape=jax.ShapeDtypeStruct((m, ATTN_WIDTH), BF16),
        compiler_params=_params(("parallel",)),
        name="attn_combine",
    )(*os, *lses)


_SGU_ROWS = 4 * CHUNK


def _sgu_kernel(u0_ref, u1_ref, v0_ref, v1_ref, g_ref, b_ref, ws_ref, bt_ref, o_ref):
    half = SGU_WIDTH // 2
    nch = v0_ref.shape[0] // CHUNK
    v0 = v0_ref[...].astype(F32)
    v1 = v1_ref[...].astype(F32)
    mu = (jnp.sum(v0, axis=-1, keepdims=True) + jnp.sum(v1, axis=-1, keepdims=True)) / SGU_WIDTH
    d0, d1 = v0 - mu, v1 - mu
    var = (jnp.sum(d0 * d0, axis=-1, keepdims=True) + jnp.sum(d1 * d1, axis=-1, keepdims=True)) / SGU_WIDTH
    inv = lax.rsqrt(var + EPS)
    vn = ((d0 * inv * g_ref[:, :half] + b_ref[:, :half]).astype(BF16),
          (d1 * inv * g_ref[:, half:] + b_ref[:, half:]).astype(BF16))
    u_refs = (u0_ref, u1_ref)
    row = lax.broadcasted_iota(jnp.int32, (CHUNK, CHUNK), 0)
    col = lax.broadcasted_iota(jnp.int32, (CHUNK, CHUNK), 1)
    tri = col <= row
    per_half = SGU_GROUPS // 2
    for gg in range(SGU_GROUPS):
        hf, gl = divmod(gg, per_half)
        sl = slice(gl * SGU_GROUP_CH, (gl + 1) * SGU_GROUP_CH)
        osl = slice(gg * SGU_GROUP_CH, (gg + 1) * SGU_GROUP_CH)
        w = jnp.where(tri, ws_ref[gg], 0.0).astype(BF16)
        v_all = jnp.concatenate([vn[hf][c * CHUNK:(c + 1) * CHUNK, sl] for c in range(nch)], axis=1)
        mixed = jnp.dot(w, v_all, preferred_element_type=F32)
        bias = bt_ref[:, gg:gg + 1]
        for c in range(nch):
            rs = slice(c * CHUNK, (c + 1) * CHUNK)
            m_c = mixed[:, c * SGU_GROUP_CH:(c + 1) * SGU_GROUP_CH] + bias
            o_ref[rs, osl] = (u_refs[hf][rs, sl].astype(F32) * m_c).astype(o_ref.dtype)


def _sgu(proj, ln_g, ln_b, w_spatial, b_spatial):
    m = proj.shape[0]
    half = SGU_WIDTH // 2
    vec = pl.BlockSpec((1, SGU_WIDTH), lambda i: (0, 0))

    def col_spec(c):
        return pl.BlockSpec((_SGU_ROWS, half), lambda i: (i, c))

    return pl.pallas_call(
        _sgu_kernel,
        grid=(m // _SGU_ROWS,),
        in_specs=[col_spec(0), col_spec(1), col_spec(2), col_spec(3),
                  vec, vec,
                  pl.BlockSpec((SGU_GROUPS, CHUNK, CHUNK), lambda i: (0, 0, 0)),
                  pl.BlockSpec((CHUNK, SGU_GROUPS), lambda i: (0, 0))],
        out_specs=pl.BlockSpec((_SGU_ROWS, SGU_WIDTH), lambda i: (i, 0)),
        out_shape=jax.ShapeDtypeStruct((m, SGU_WIDTH), BF16),
        compiler_params=_params(("parallel",)),
        name="sgu",
    )(proj, proj, proj, proj, ln_g.reshape(1, -1), ln_b.reshape(1, -1), w_spatial, b_spatial.T)


def _merge_kernel(h_ref, ya_ref, yb_ref, wga_ref, wgb_ref, wa_ref, wb_ref, bga_ref, bgb_ref, o_ref):
    h = h_ref[...]
    ga = jax.nn.sigmoid(jnp.dot(h, wga_ref[...], preferred_element_type=F32) + bga_ref[...])
    a = jnp.dot(ya_ref[...], wa_ref[...], preferred_element_type=F32)
    acc = ga * a
    gb = jax.nn.sigmoid(jnp.dot(h, wgb_ref[...], preferred_element_type=F32) + bgb_ref[...])
    b = jnp.dot(yb_ref[...], wb_ref[...], preferred_element_type=F32)
    o_ref[...] = (acc + gb * b).astype(o_ref.dtype)


def _merge(h, ya, yb, w_gate, b_gate, w_a, w_b, cast_ws, tm=512, tn=512):
    m = h.shape[0]
    nb = D_MODEL // tn
    b_gate = b_gate.reshape(1, -1)
    return _call_with_casts(
        _merge_kernel, cast_ws,
        grid=(m // tm, nb),
        in_specs=[pl.BlockSpec((tm, D_MODEL), lambda i, j: (i, 0)),
                  pl.BlockSpec((tm, ATTN_WIDTH), lambda i, j: (i, 0)),
                  pl.BlockSpec((tm, SGU_WIDTH), lambda i, j: (i, 0)),
                  pl.BlockSpec((D_MODEL, tn), lambda i, j: (0, j)),
                  pl.BlockSpec((D_MODEL, tn), lambda i, j: (0, j + nb)),
                  pl.BlockSpec((ATTN_WIDTH, tn), lambda i, j: (0, j)),
                  pl.BlockSpec((SGU_WIDTH, tn), lambda i, j: (0, j)),
                  pl.BlockSpec((1, tn), lambda i, j: (0, j)),
                  pl.BlockSpec((1, tn), lambda i, j: (0, j + nb))],
        out_specs=[pl.BlockSpec((tm, tn), lambda i, j: (i, j))],
        out_shape=[jax.ShapeDtypeStruct((m, D_MODEL), BF16)],
        args=(h, ya, yb, w_gate, w_gate, w_a, w_b, b_gate, b_gate),
        compiler_params=_params(("parallel", "arbitrary")),
        name="gated_merge",
    )


def _matmul_kernel(a_ref, w_ref, o_ref, *, act):
    acc = jnp.dot(a_ref[...], w_ref[...], preferred_element_type=F32)
    if act == "relu2":
        r = jnp.maximum(acc, 0.0)
        acc = r * r
    o_ref[...] = acc.astype(o_ref.dtype)


def _matmul(a, w, out_dtype, act=None, tm=1024, tn=1024, name="matmul", cast_ws=()):
    m, k = a.shape
    n = w.shape[1]
    tm, tn = min(tm, m), min(tn, n)
    return _call_with_casts(
        functools.partial(_matmul_kernel, act=act), cast_ws,
        grid=(m // tm, n // tn),
        in_specs=[pl.BlockSpec((tm, k), lambda i, j: (i, 0)), pl.BlockSpec((k, tn), lambda i, j: (0, j))],
        out_specs=[pl.BlockSpec((tm, tn), lambda i, j: (i, j))],
        out_shape=[jax.ShapeDtypeStruct((m, n), out_dtype)],
        args=(a, w),
        compiler_params=_params(("parallel", "arbitrary")),
        name=name,
    )


_NORM_DOT_CHUNK = 1024


def _matmul_norm_kernel(a_ref, w_ref, x_ref, g_ref, xo_ref, xs_ref):
    k = pl.program_id(1)
    d = xo_ref.shape[1]
    xw = x_ref.shape[1]
    xs_ref[:, pl.ds(pl.multiple_of(k * xw, xw), xw)] = x_ref[...]
    a = a_ref[...]
    for c in range(d // _NORM_DOT_CHUNK):
        sl = slice(c * _NORM_DOT_CHUNK, (c + 1) * _NORM_DOT_CHUNK)
        part = jnp.dot(a, w_ref[:, sl], preferred_element_type=F32)
        xo_ref[:, sl] = jnp.where(k == 0, 0.0, xo_ref[:, sl]) + part

    @pl.when(k == pl.num_programs(1) - 1)
    def _():
        xo_ref[...] = xs_ref[...] + _rms(xo_ref[...], g_ref[...])


def _matmul_norm_residual(a, w, x, g, name, tm=512, tk=1024):
    m, kdim = a.shape
    d = w.shape[1]
    nk = kdim // tk
    return pl.pallas_call(
        _matmul_norm_kernel,
        grid=(m // tm, nk),
        in_specs=[pl.BlockSpec((tm, tk), lambda i, k: (i, k)),
                  pl.BlockSpec((tk, d), lambda i, k: (k, 0)),
                  pl.BlockSpec((tm, d // nk), lambda i, k: (i, k)),
                  pl.BlockSpec((1, d), lambda i, k: (0, 0))],
        out_specs=pl.BlockSpec((tm, d), lambda i, k: (i, 0)),
        out_shape=jax.ShapeDtypeStruct((m, d), F32),
        scratch_shapes=[pltpu.VMEM((tm, d), F32)],
        compiler_params=_params(("parallel", "arbitrary")),
        name=name,
    )(a, w, x, g.reshape(1, d))


def _xattn_kernel(x_ref, gpre_ref, wq_ref, k_ref, v_ref, wo_ref, gpost_ref, gnext_ref, xo_ref, hn_ref):
    x = x_ref[...]
    h = _rms(x, gpre_ref[...]).astype(BF16)
    q = jnp.dot(h, wq_ref[...], preferred_element_type=F32) * ATTN_SCALE
    q = q.astype(BF16)
    dn = (((1,), (1,)), ((), ()))
    outs = []
    for hh in range(XA_HEADS):
        sl = slice(hh * HEAD_DIM, (hh + 1) * HEAD_DIM)
        s = lax.dot_general(q[:, sl], k_ref[:, sl], dn, preferred_element_type=F32)
        mx = jnp.max(s, axis=-1, keepdims=True)
        p = jnp.exp(s - mx)
        den = jnp.sum(p, axis=-1, keepdims=True)
        o = jnp.dot(p.astype(BF16), v_ref[:, sl], preferred_element_type=F32)
        outs.append((o / den).astype(BF16))
    o_all = jnp.concatenate(outs, axis=-1)
    y = jnp.dot(o_all, wo_ref[...], preferred_element_type=F32)
    x_new = x + _rms(y, gpost_ref[...])
    xo_ref[...] = x_new
    hn_ref[...] = _rms(x_new, gnext_ref[...]).astype(hn_ref.dtype)


def _xattn(x, g_pre, w_xq, k_mem, v_mem, w_xo, g_post, g_next, batch, seq, tm=256):
    m, d = x.shape
    n_mem = k_mem.shape[0] // batch
    per_b = seq // tm
    row = pl.BlockSpec((tm, d), lambda i: (i, 0))
    vec = pl.BlockSpec((1, d), lambda i: (0, 0))
    return pl.pallas_call(
        _xattn_kernel,
        grid=(m // tm,),
        in_specs=[row, vec,
                  pl.BlockSpec((d, XA_WIDTH), lambda i: (0, 0)),
                  pl.BlockSpec((n_mem, XA_WIDTH), lambda i: (i // per_b, 0)),
                  pl.BlockSpec((n_mem, XA_WIDTH), lambda i: (i // per_b, 0)),
                  pl.BlockSpec((XA_WIDTH, d), lambda i: (0, 0)),
                  vec, vec],
        out_specs=[row, row],
        out_shape=[jax.ShapeDtypeStruct((m, d), F32), jax.ShapeDtypeStruct((m, d), BF16)],
        compiler_params=_params(("parallel",)),
        name="cross_attn",
    )(x, g_pre.reshape(1, d), w_xq, k_mem, v_mem, w_xo, g_post.reshape(1, d), g_next.reshape(1, d))


def kernel(x, mem, positions, mix_pre_g, w_in, sgu_ln_g, sgu_ln_b, w_spatial, b_spatial, w_branch_a, w_branch_b, w_gate, b_gate, w_out, mix_post_g, xa_pre_g, mem_norm_g, w_xq, w_xk, w_xv, w_xo, xa_post_g, mlp_pre_g, w_up, w_down, mlp_post_g):
    batch, seq, d = x.shape
    n_mem = mem.shape[1]
    depth = w_in.shape[0]
    m = batch * seq
    x2 = x.reshape(m, d)
    mem2 = mem.reshape(batch * n_mem, d)
    c_tab, sa_tab, sb_tab = _rope_tables(positions)

    for l in range(depth):
        w_in_bf = w_in[l].astype(BF16)

        h = _rmsnorm(x2, mix_pre_g[l])
        uv, w_gate_bf, w_a_bf, w_b_bf = _inproj_uv(h, w_in_bf, (w_gate[l], w_branch_a[l], w_branch_b[l]))
        os, lses = [], []
        for g in range(N_GROUPS):
            qkv = _inproj_qkv(h, w_in_bf, c_tab, sa_tab, sb_tab, g, batch, seq)
            o, lse = _band_attention(qkv, g, batch, seq)
            os.append(o)
            lses.append(lse)
        y_a = _combine(os, lses)
        y_b = _sgu(uv, sgu_ln_g[l], sgu_ln_b[l], w_spatial[l], b_spatial[l])
        merged, w_up_bf, w_out_bf, w_xq_bf, w_xk_bf, w_xv_bf, w_xo_bf = _merge(
            h, y_a, y_b, w_gate_bf, b_gate[l], w_a_bf, w_b_bf,
            (w_up[l], w_out[l], w_xq[l], w_xk[l], w_xv[l], w_xo[l]))
        x2 = _matmul_norm_residual(merged, w_out_bf, x2, mix_post_g[l], "out_proj_norm")

        mn = _rmsnorm(mem2, mem_norm_g[l])
        k_mem, = _matmul(mn, w_xk_bf, BF16, name="mem_k")
        v_mem, = _matmul(mn, w_xv_bf, BF16, name="mem_v")
        x2, h = _xattn(x2, xa_pre_g[l], w_xq_bf, k_mem, v_mem, w_xo_bf, xa_post_g[l], mlp_pre_g[l], batch, seq)

        a, w_down_bf = _matmul(h, w_up_bf, BF16, act="relu2", name="mlp_up", cast_ws=(w_down[l],))
        x2 = _matmul_norm_residual(a, w_down_bf, x2, mlp_post_g[l], "mlp_down_norm")
    return x2.reshape(batch, seq, d)
```

```python
import functools
import math

import jax
import jax.numpy as jnp
from jax import lax
from jax.experimental import pallas as pl
from jax.experimental.pallas import tpu as pltpu

F32 = jnp.float32
BF16 = jnp.bfloat16

D_MODEL = 4096
HEAD_DIM = 128
ATTN_GROUPS = ((128, 1), (512, 4), (2048, 16))
N_GROUPS = len(ATTN_GROUPS)
HEADS_PER_GROUP = 8
ATTN_WIDTH = HEADS_PER_GROUP * HEAD_DIM
ATTN_QKV_WIDTH = N_GROUPS * ATTN_WIDTH
BAND = 128
SGU_WIDTH = D_MODEL // 2
SGU_GROUP_CH = 128
SGU_GROUPS = SGU_WIDTH // SGU_GROUP_CH
CHUNK = 128
IN_WIDTH = 3 * ATTN_QKV_WIDTH + 2 * SGU_WIDTH
ROPE_THETA = 500000.0
ROT_DIM = HEAD_DIM // 4
ROT_HALF = ROT_DIM // 2
XA_HEADS = 4
XA_WIDTH = XA_HEADS * HEAD_DIM
D_FF = 4 * D_MODEL
EPS = 1e-6
NEG = -1e30
ATTN_SCALE = HEAD_DIM ** -0.5
LANES = 128
BF16_SUBLANES = 16

VMEM_LIMIT = 56 * 1024 * 1024
_XATTN_VMEM_LIMIT = 58 * 1024 * 1024


def _params(sem, vmem=VMEM_LIMIT):
    return pltpu.CompilerParams(dimension_semantics=sem, vmem_limit_bytes=vmem)


def _gelu_tanh(x):
    c = math.sqrt(2.0 / math.pi)
    return 0.5 * x * (1.0 + jnp.tanh(c * (x + 0.044715 * (x * x * x))))


def _rms(v, g):
    return v * lax.rsqrt(jnp.mean(v * v, axis=-1, keepdims=True) + EPS) * g


def _cast_spec(w, grid):
    rows, cols = w.shape
    nblk = math.prod(grid)
    while rows % nblk or (rows // nblk) % BF16_SUBLANES:
        nblk -= 1

    def imap(*idx):
        step = idx[0]
        for extent, ix in zip(grid[1:], idx[1:]):
            step = step * extent + ix
        return jnp.minimum(step, nblk - 1), 0

    return pl.BlockSpec((rows // nblk, cols), imap)


def _call_with_casts(body, cast_ws, *, grid, in_specs, out_specs, out_shape, args, **kw):
    n_in, n_out, n_cast = len(in_specs), len(out_specs), len(cast_ws)

    def kern(*refs):
        ins, refs = refs[:n_in], refs[n_in:]
        cast_in, refs = refs[:n_cast], refs[n_cast:]
        outs, refs = refs[:n_out], refs[n_out:]
        cast_out, scratch = refs[:n_cast], refs[n_cast:]
        for src, dst in zip(cast_in, cast_out):
            dst[...] = src[...].astype(dst.dtype)
        body(*ins, *outs, *scratch)

    cspecs = [_cast_spec(w, grid) for w in cast_ws]
    return pl.pallas_call(
        kern,
        grid=grid,
        in_specs=list(in_specs) + cspecs,
        out_specs=list(out_specs) + cspecs,
        out_shape=list(out_shape) + [jax.ShapeDtypeStruct(w.shape, BF16) for w in cast_ws],
        **kw,
    )(*args, *cast_ws)


def _rope_table_kernel(pos_ref, inv_ref, c_ref, sa_ref, sb_ref):
    ang = pos_ref[...] * inv_ref[...]
    c = jnp.cos(ang)
    s = jnp.sin(ang)
    lane = lax.broadcasted_iota(jnp.int32, ang.shape, 1)
    c_ref[...] = jnp.where(lane < ROT_DIM, c, 1.0)
    sa_ref[...] = jnp.where(lane < ROT_HALF, -s, 0.0)
    sb_ref[...] = jnp.where((lane >= ROT_HALF) & (lane < ROT_DIM), s, 0.0)


def _rope_tables(positions):
    m = positions.size
    tm = 2048
    inv = ROPE_THETA ** (-jnp.arange(0, ROT_DIM, 2, dtype=F32) / ROT_DIM)
    inv_lane = jnp.concatenate([inv, inv, jnp.zeros((LANES - ROT_DIM,), F32)])[None, :]
    pos = positions.astype(F32).reshape(m, 1)
    out = jax.ShapeDtypeStruct((m, LANES), F32)
    spec = pl.BlockSpec((tm, LANES), lambda i: (i, 0))
    return pl.pallas_call(
        _rope_table_kernel,
        grid=(m // tm,),
        in_specs=[pl.BlockSpec((tm, 1), lambda i: (i, 0)), pl.BlockSpec((1, LANES), lambda i: (0, 0))],
        out_specs=[spec, spec, spec],
        out_shape=[out, out, out],
        compiler_params=_params(("parallel",)),
        name="rope_tables",
    )(pos, inv_lane)


def _rmsnorm_kernel(x_ref, g_ref, o_ref):
    o_ref[...] = _rms(x_ref[...], g_ref[...]).astype(o_ref.dtype)


def _rmsnorm(x2d, g, out_dtype=BF16, tm=512):
    m, d = x2d.shape
    return pl.pallas_call(
        _rmsnorm_kernel,
        grid=(m // tm,),
        in_specs=[pl.BlockSpec((tm, d), lambda i: (i, 0)), pl.BlockSpec((1, d), lambda i: (0, 0))],
        out_specs=pl.BlockSpec((tm, d), lambda i: (i, 0)),
        out_shape=jax.ShapeDtypeStruct((m, d), out_dtype),
        compiler_params=_params(("parallel",)),
        name="rmsnorm",
    )(x2d, g.reshape(1, d))


_UV_TILES = 2 * SGU_WIDTH // ATTN_WIDTH
_MAX_FAST_STRIDE = 4
_DOT_CHUNK = 2 * HEAD_DIM


def _pipelined_tiles(dot_into, finish, acc_a, acc_b):
    s = pl.program_id(0)
    last = pl.num_programs(0) - 1

    @pl.when(s == 0)
    def _():
        acc_b[...] = jnp.zeros_like(acc_b)

    @pl.when((s % 2 == 0) & (s < last))
    def _():
        dot_into(acc_a)
        finish(acc_b)

    @pl.when(s % 2 == 1)
    def _():
        dot_into(acc_b)
        finish(acc_a)

    @pl.when(s == last)
    def _():
        finish(acc_b)


def _dot_into_heads(h_ref, w_ref, acc):
    h = h_ref[...]
    for cc in range(ATTN_WIDTH // _DOT_CHUNK):
        part = jnp.dot(h, w_ref[:, cc * _DOT_CHUNK:(cc + 1) * _DOT_CHUNK], preferred_element_type=F32)
        for hc in range(_DOT_CHUNK // HEAD_DIM):
            acc[cc * (_DOT_CHUNK // HEAD_DIM) + hc] = part[:, hc * HEAD_DIM:(hc + 1) * HEAD_DIM]


def _inproj_qkv_kernel(h_ref, w_ref, c_ref, sa_ref, sb_ref, o_ref, acc_a, acc_b, *scratch, dil):
    tm = h_ref.shape[0]

    def finish(acc):
        kind = jnp.maximum(pl.program_id(0) - 1, 0) % 3
        c = c_ref[...]
        sa = sa_ref[...]
        sb = sb_ref[...]
        rotate = kind < 2
        scale = jnp.where(kind == 0, ATTN_SCALE, 1.0).astype(F32)
        for hh in range(HEADS_PER_GROUP):
            osl = slice(hh * HEAD_DIM, (hh + 1) * HEAD_DIM)
            xh = acc[hh]
            y = xh * c + pltpu.roll(xh, LANES - ROT_HALF, 1) * sa + pltpu.roll(xh, ROT_HALF, 1) * sb
            y = jnp.where(rotate, y * scale, xh)
            if dil == 1:
                o_ref[0, :, osl] = y.astype(o_ref.dtype)
                continue
            acc[hh] = y
            if dil <= _MAX_FAST_STRIDE:
                for r in range(dil):
                    rows = acc[hh, pl.ds(r, tm // dil, stride=dil), :]
                    o_ref[r, :, osl] = rows.astype(o_ref.dtype)
            else:
                outer = dil // _MAX_FAST_STRIDE
                for r_lo in range(_MAX_FAST_STRIDE):
                    scratch[0][hh] = acc[hh, pl.ds(r_lo, tm // _MAX_FAST_STRIDE, stride=_MAX_FAST_STRIDE), :]
                    for r_hi in range(outer):
                        rows = scratch[0][hh, pl.ds(r_hi, tm // dil, stride=outer), :]
                        o_ref[r_hi * _MAX_FAST_STRIDE + r_lo, :, osl] = rows.astype(o_ref.dtype)

    _pipelined_tiles(functools.partial(_dot_into_heads, h_ref, w_ref), finish, acc_a, acc_b)


def _inproj_uv_kernel(h_ref, w_ref, o_ref):
    h = h_ref[...]
    for cc in range(ATTN_WIDTH // _DOT_CHUNK):
        sl = slice(cc * _DOT_CHUNK, (cc + 1) * _DOT_CHUNK)
        acc = jnp.dot(h, w_ref[:, sl], preferred_element_type=F32)
        o_ref[:, sl] = _gelu_tanh(acc).astype(o_ref.dtype)


def _tile_maps(n_tiles):
    assert n_tiles % 2 == 0
    return (lambda s: jnp.minimum(s, n_tiles - 1)), (lambda s: jnp.maximum(s - 1, 0))


def _inproj_qkv(h, w_in_bf16, c_tab, sa_tab, sb_tab, g, batch, seq, tm=1024):
    m, k = h.shape
    tn = ATTN_WIDTH
    dil = ATTN_GROUPS[g][1]
    per_b = seq // tm
    n_tiles = (m // tm) * 3
    dot_tile, fin_tile = _tile_maps(n_tiles)

    def out_map(s):
        i, kind = fin_tile(s) // 3, fin_tile(s) % 3
        return i // per_b, 0, i % per_b, kind

    tab_spec = pl.BlockSpec((tm, LANES), lambda s: (fin_tile(s) // 3, 0))
    acc = pltpu.VMEM((HEADS_PER_GROUP, tm, HEAD_DIM), F32)
    scratch = [acc, acc]
    if dil > _MAX_FAST_STRIDE:
        scratch.append(pltpu.VMEM((HEADS_PER_GROUP, tm // _MAX_FAST_STRIDE, HEAD_DIM), F32))
    return pl.pallas_call(
        functools.partial(_inproj_qkv_kernel, dil=dil),
        grid=(n_tiles + 1,),
        in_specs=[pl.BlockSpec((tm, k), lambda s: (dot_tile(s) // 3, 0)),
                  pl.BlockSpec((k, tn), lambda s: (0, (dot_tile(s) % 3) * N_GROUPS + g)),
                  tab_spec, tab_spec, tab_spec],
        out_specs=pl.BlockSpec((None, dil, tm // dil, tn), out_map),
        out_shape=jax.ShapeDtypeStruct((batch, dil, seq // dil, 3 * tn), BF16),
        scratch_shapes=scratch,
        compiler_params=_params(("arbitrary",)),
        name=f"inproj_qkv_g{g}",
    )(h, w_in_bf16, c_tab, sa_tab, sb_tab)


def _inproj_uv(h, w_in_bf16, cast_ws, tm=1024):
    m, k = h.shape
    tn = ATTN_WIDTH
    first = 3 * ATTN_QKV_WIDTH // tn
    return _call_with_casts(
        _inproj_uv_kernel, cast_ws,
        grid=(m // tm, _UV_TILES),
        in_specs=[pl.BlockSpec((tm, k), lambda i, j: (i, 0)),
                  pl.BlockSpec((k, tn), lambda i, j: (0, first + j))],
        out_specs=[pl.BlockSpec((tm, tn), lambda i, j: (i, j))],
        out_shape=[jax.ShapeDtypeStruct((m, _UV_TILES * tn), BF16)],
        args=(h, w_in_bf16),
        compiler_params=_params(("parallel", "arbitrary")),
        name="inproj_uv",
    )


_ATTN_QBLOCKS = 2


def _band_attn_kernel(q_ref, kp_ref, kc_ref, vp_ref, vc_ref, o_ref, lse_ref, *scratch, dil):
    i = pl.program_id(1)
    r = pl.program_id(2)
    row = lax.broadcasted_iota(jnp.int32, (BAND, BAND), 0)
    col = lax.broadcasted_iota(jnp.int32, (BAND, BAND), 1)
    mask_cur = col <= row
    mask_band = col >= row

    def heads(ref, blk):
        rows = slice(blk * BAND, (blk + 1) * BAND)
        return jnp.stack([ref[rows, hh * HEAD_DIM:(hh + 1) * HEAD_DIM] for hh in range(HEADS_PER_GROUP)])

    for blk in range(_ATTN_QBLOCKS):
        q, k_c, v_c = heads(q_ref, blk), heads(kc_ref, blk), heads(vc_ref, blk)
        if blk == 0:
            k_p, v_p = heads(kp_ref, 0), heads(vp_ref, 0)
            mask_prev = mask_band & (i > 0)
        else:
            k_p, v_p = heads(kc_ref, blk - 1), heads(vc_ref, blk - 1)
            mask_prev = mask_band
        s_c = jnp.einsum("hqd,hkd->hqk", q, k_c, preferred_element_type=F32)
        s_p = jnp.einsum("hqd,hkd->hqk", q, k_p, preferred_element_type=F32)
        s_c = jnp.where(mask_cur, s_c, NEG)
        s_p = jnp.where(mask_prev, s_p, NEG)
        mx = jnp.maximum(jnp.max(s_c, axis=-1, keepdims=True), jnp.max(s_p, axis=-1, keepdims=True))
        p_c = jnp.exp(s_c - mx)
        p_p = jnp.exp(s_p - mx)
        den = jnp.sum(p_c, axis=-1, keepdims=True) + jnp.sum(p_p, axis=-1, keepdims=True)
        o = jnp.einsum("hqk,hkd->hqd", p_c.astype(BF16), v_c, preferred_element_type=F32)
        o = o + jnp.einsum("hqk,hkd->hqd", p_p.astype(BF16), v_p, preferred_element_type=F32)
        o = o / den
        lse = mx + jnp.log(den)
        lse_tile = jnp.zeros((BAND, LANES), F32)
        rows = slice(blk * BAND, (blk + 1) * BAND) if dil == 1 else pl.ds(blk * BAND * dil + r, BAND, stride=dil)
        for hh in range(HEADS_PER_GROUP):
            if dil == 1:
                o_ref[rows, hh * HEAD_DIM:(hh + 1) * HEAD_DIM] = o[hh].astype(o_ref.dtype)
            else:
                scratch[0][hh, rows, :] = o[hh]
            lse_tile = jnp.where(col == hh, lse[hh], lse_tile)
        lse_ref[rows, :] = lse_tile

    if dil > 1:
        @pl.when(r == dil - 1)
        def _():
            for hh in range(HEADS_PER_GROUP):
                o_ref[:, hh * HEAD_DIM:(hh + 1) * HEAD_DIM] = scratch[0][hh].astype(o_ref.dtype)


def _band_attention(qkv, g, batch, seq):
    _, dil = ATTN_GROUPS[g]
    sub = seq // dil
    qrows = _ATTN_QBLOCKS * BAND
    nstep = sub // qrows
    rows = qrows * dil

    def cur(col):
        return pl.BlockSpec((None, None, qrows, ATTN_WIDTH), lambda b, i, r: (b, r, i, col))

    def prev(col):
        return pl.BlockSpec((None, None, BAND, ATTN_WIDTH),
                            lambda b, i, r: (b, r, jnp.maximum(i * _ATTN_QBLOCKS - 1, 0), col))

    return pl.pallas_call(
        functools.partial(_band_attn_kernel, dil=dil),
        grid=(batch, nstep, dil),
        in_specs=[cur(0), prev(1), cur(1), prev(2), cur(2)],
        out_specs=[pl.BlockSpec((rows, ATTN_WIDTH), lambda b, i, r: (b * nstep + i, 0)),
                   pl.BlockSpec((rows, LANES), lambda b, i, r: (b * nstep + i, 0))],
        out_shape=[jax.ShapeDtypeStruct((batch * seq, ATTN_WIDTH), BF16),
                   jax.ShapeDtypeStruct((batch * seq, LANES), F32)],
        scratch_shapes=[pltpu.VMEM((HEADS_PER_GROUP, rows, HEAD_DIM), F32)] if dil > 1 else [],
        compiler_params=_params(("parallel", "parallel", "arbitrary")),
        name=f"band_attn_g{g}",
    )(qkv, qkv, qkv, qkv, qkv)


def _combine_kernel(o0_ref, o1_ref, o2_ref, l0_ref, l1_ref, l2_ref, y_ref):
    l0, l1, l2 = l0_ref[...], l1_ref[...], l2_ref[...]
    mx = jnp.maximum(jnp.maximum(l0, l1), l2)
    e0, e1, e2 = jnp.exp(l0 - mx), jnp.exp(l1 - mx), jnp.exp(l2 - mx)
    den = e0 + e1 + e2
    a0, a1 = e0 / den, e1 / den
    rows = l0.shape[0]
    for hh in range(HEADS_PER_GROUP):
        sl = slice(hh * HEAD_DIM, (hh + 1) * HEAD_DIM)
        b0 = jnp.broadcast_to(a0[:, hh:hh + 1], (rows, HEAD_DIM))
        b1 = jnp.broadcast_to(a1[:, hh:hh + 1], (rows, HEAD_DIM))
        o2 = o2_ref[:, sl].astype(F32)
        y = o2 + b0 * (o0_ref[:, sl].astype(F32) - o2) + b1 * (o1_ref[:, sl].astype(F32) - o2)
        y_ref[:, sl] = y.astype(y_ref.dtype)


def _combine(os, lses, tm=512):
    m = os[0].shape[0]
    ospec = pl.BlockSpec((tm, ATTN_WIDTH), lambda i: (i, 0))
    lspec = pl.BlockSpec((tm, LANES), lambda i: (i, 0))
    return pl.pallas_call(
        _combine_kernel,
        grid=(m // tm,),
        in_specs=[ospec, ospec, ospec, lspec, lspec, lspec],
        out_specs=ospec,
        out_shape=jax.ShapeDtypeStruct((m, ATTN_WIDTH), BF16),
        compiler_params=_params(("parallel",)),
        name="attn_combine",
    )(*os, *lses)


_SGU_ROWS = 4 * CHUNK


def _sgu_kernel(u0_ref, u1_ref, v0_ref, v1_ref, g_ref, b_ref, ws_ref, bt_ref, o_ref):
    half = SGU_WIDTH // 2
    nch = v0_ref.shape[0] // CHUNK
    v0 = v0_ref[...].astype(F32)
    v1 = v1_ref[...].astype(F32)
    mu = (jnp.sum(v0, axis=-1, keepdims=True) + jnp.sum(v1, axis=-1, keepdims=True)) / SGU_WIDTH
    d0, d1 = v0 - mu, v1 - mu
    var = (jnp.sum(d0 * d0, axis=-1, keepdims=True) + jnp.sum(d1 * d1, axis=-1, keepdims=True)) / SGU_WIDTH
    inv = lax.rsqrt(var + EPS)
    vn = ((d0 * inv * g_ref[:, :half] + b_ref[:, :half]).astype(BF16),
          (d1 * inv * g_ref[:, half:] + b_ref[:, half:]).astype(BF16))
    u_refs = (u0_ref, u1_ref)
    row = lax.broadcasted_iota(jnp.int32, (CHUNK, CHUNK), 0)
    col = lax.broadcasted_iota(jnp.int32, (CHUNK, CHUNK), 1)
    tri = col <= row
    per_half = SGU_GROUPS // 2
    for gg in range(SGU_GROUPS):
        hf, gl = divmod(gg, per_half)
        sl = slice(gl * SGU_GROUP_CH, (gl + 1) * SGU_GROUP_CH)
        osl = slice(gg * SGU_GROUP_CH, (gg + 1) * SGU_GROUP_CH)
        w = jnp.where(tri, ws_ref[gg], 0.0).astype(BF16)
        v_all = jnp.concatenate([vn[hf][c * CHUNK:(c + 1) * CHUNK, sl] for c in range(nch)], axis=1)
        mixed = jnp.dot(w, v_all, preferred_element_type=F32)
        bias = bt_ref[:, gg:gg + 1]
        for c in range(nch):
            rs = slice(c * CHUNK, (c + 1) * CHUNK)
            m_c = mixed[:, c * SGU_GROUP_CH:(c + 1) * SGU_GROUP_CH] + bias
            o_ref[rs, osl] = (u_refs[hf][rs, sl].astype(F32) * m_c).astype(o_ref.dtype)


def _sgu(proj, ln_g, ln_b, w_spatial, b_spatial):
    m = proj.shape[0]
    half = SGU_WIDTH // 2
    vec = pl.BlockSpec((1, SGU_WIDTH), lambda i: (0, 0))

    def col_spec(c):
        return pl.BlockSpec((_SGU_ROWS, half), lambda i: (i, c))

    return pl.pallas_call(
        _sgu_kernel,
        grid=(m // _SGU_ROWS,),
        in_specs=[col_spec(0), col_spec(1), col_spec(2), col_spec(3),
                  vec, vec,
                  pl.BlockSpec((SGU_GROUPS, CHUNK, CHUNK), lambda i: (0, 0, 0)),
                  pl.BlockSpec((CHUNK, SGU_GROUPS), lambda i: (0, 0))],
        out_specs=pl.BlockSpec((_SGU_ROWS, SGU_WIDTH), lambda i: (i, 0)),
        out_shape=jax.ShapeDtypeStruct((m, SGU_WIDTH), BF16),
        compiler_params=_params(("parallel",)),
        name="sgu",
    )(proj, proj, proj, proj, ln_g.reshape(1, -1), ln_b.reshape(1, -1), w_spatial, b_spatial.T)


def _merge_kernel(h_ref, ya_ref, yb_ref, wga_ref, wgb_ref, wa_ref, wb_ref, bga_ref, bgb_ref, o_ref):
    h = h_ref[...]
    ga = jax.nn.sigmoid(jnp.dot(h, wga_ref[...], preferred_element_type=F32) + bga_ref[...])
    a = jnp.dot(ya_ref[...], wa_ref[...], preferred_element_type=F32)
    acc = ga * a
    gb = jax.nn.sigmoid(jnp.dot(h, wgb_ref[...], preferred_element_type=F32) + bgb_ref[...])
    b = jnp.dot(yb_ref[...], wb_ref[...], preferred_element_type=F32)
    o_ref[...] = (acc + gb * b).astype(o_ref.dtype)


def _merge(h, ya, yb, w_gate, b_gate, w_a, w_b, cast_ws, tm=512, tn=512):
    m = h.shape[0]
    nb = D_MODEL // tn
    b_gate = b_gate.reshape(1, -1)
    return _call_with_casts(
        _merge_kernel, cast_ws,
        grid=(m // tm, nb),
        in_specs=[pl.BlockSpec((tm, D_MODEL), lambda i, j: (i, 0)),
                  pl.BlockSpec((tm, ATTN_WIDTH), lambda i, j: (i, 0)),
                  pl.BlockSpec((tm, SGU_WIDTH), lambda i, j: (i, 0)),
                  pl.BlockSpec((D_MODEL, tn), lambda i, j: (0, j)),
                  pl.BlockSpec((D_MODEL, tn), lambda i, j: (0, j + nb)),
                  pl.BlockSpec((ATTN_WIDTH, tn), lambda i, j: (0, j)),
                  pl.BlockSpec((SGU_WIDTH, tn), lambda i, j: (0, j)),
                  pl.BlockSpec((1, tn), lambda i, j: (0, j)),
                  pl.BlockSpec((1, tn), lambda i, j: (0, j + nb))],
        out_specs=[pl.BlockSpec((tm, tn), lambda i, j: (i, j))],
        out_shape=[jax.ShapeDtypeStruct((m, D_MODEL), BF16)],
        args=(h, ya, yb, w_gate, w_gate, w_a, w_b, b_gate, b_gate),
        compiler_params=_params(("parallel", "arbitrary")),
        name="gated_merge",
    )


def _matmul_kernel(a_ref, w_ref, o_ref, *, act):
    acc = jnp.dot(a_ref[...], w_ref[...], preferred_element_type=F32)
    if act == "relu2":
        r = jnp.maximum(acc, 0.0)
        acc = r * r
    o_ref[...] = acc.astype(o_ref.dtype)


def _matmul(a, w, out_dtype, act=None, tm=1024, tn=1024, name="matmul", cast_ws=()):
    m, k = a.shape
    n = w.shape[1]
    tm, tn = min(tm, m), min(tn, n)
    return _call_with_casts(
        functools.partial(_matmul_kernel, act=act), cast_ws,
        grid=(m // tm, n // tn),
        in_specs=[pl.BlockSpec((tm, k), lambda i, j: (i, 0)), pl.BlockSpec((k, tn), lambda i, j: (0, j))],
        out_specs=[pl.BlockSpec((tm, tn), lambda i, j: (i, j))],
        out_shape=[jax.ShapeDtypeStruct((m, n), out_dtype)],
        args=(a, w),
        compiler_params=_params(("parallel", "arbitrary")),
        name=name,
    )


_NORM_DOT_CHUNK = 1024


def _matmul_norm_kernel(a_ref, w_ref, x_ref, g_ref, xo_ref, xs_ref):
    k = pl.program_id(1)
    d = xo_ref.shape[1]
    xw = x_ref.shape[1]
    xs_ref[:, pl.ds(pl.multiple_of(k * xw, xw), xw)] = x_ref[...]
    a = a_ref[...]
    for c in range(d // _NORM_DOT_CHUNK):
        sl = slice(c * _NORM_DOT_CHUNK, (c + 1) * _NORM_DOT_CHUNK)
        part = jnp.dot(a, w_ref[:, sl], preferred_element_type=F32)
        xo_ref[:, sl] = jnp.where(k == 0, 0.0, xo_ref[:, sl]) + part

    @pl.when(k == pl.num_programs(1) - 1)
    def _():
        xo_ref[...] = xs_ref[...] + _rms(xo_ref[...], g_ref[...])


def _matmul_norm_residual(a, w, x, g, name, tm=512, tk=1024):
    m, kdim = a.shape
    d = w.shape[1]
    nk = kdim // tk
    return pl.pallas_call(
        _matmul_norm_kernel,
        grid=(m // tm, nk),
        in_specs=[pl.BlockSpec((tm, tk), lambda i, k: (i, k)),
                  pl.BlockSpec((tk, d), lambda i, k: (k, 0)),
                  pl.BlockSpec((tm, d // nk), lambda i, k: (i, k)),
                  pl.BlockSpec((1, d), lambda i, k: (0, 0))],
        out_specs=pl.BlockSpec((tm, d), lambda i, k: (i, 0)),
        out_shape=jax.ShapeDtypeStruct((m, d), F32),
        scratch_shapes=[pltpu.VMEM((tm, d), F32)],
        compiler_params=_params(("parallel", "arbitrary")),
        name=name,
    )(a, w, x, g.reshape(1, d))


def _xattn_kernel(x_ref, gpre_ref, wq_ref, k_ref, v_ref, wo_ref, gpost_ref, gnext_ref, xo_ref, hn_ref):
    x = x_ref[...]
    h = _rms(x, gpre_ref[...]).astype(BF16)
    q = jnp.dot(h, wq_ref[...], preferred_element_type=F32) * ATTN_SCALE
    q = q.astype(BF16)
    dn = (((1,), (1,)), ((), ()))
    outs = []
    for hh in range(XA_HEADS):
        sl = slice(hh * HEAD_DIM, (hh + 1) * HEAD_DIM)
        s = lax.dot_general(q[:, sl], k_ref[:, sl], dn, preferred_element_type=F32)
        mx = jnp.max(s, axis=-1, keepdims=True)
        p = jnp.exp(s - mx)
        den = jnp.sum(p, axis=-1, keepdims=True)
        o = jnp.dot(p.astype(BF16), v_ref[:, sl], preferred_element_type=F32)
        outs.append((o / den).astype(BF16))
    o_all = jnp.concatenate(outs, axis=-1)
    y = jnp.dot(o_all, wo_ref[...], preferred_element_type=F32)
    x_new = x + _rms(y, gpost_ref[...])
    xo_ref[...] = x_new
    hn_ref[...] = _rms(x_new, gnext_ref[...]).astype(hn_ref.dtype)


def _xattn(x, g_pre, w_xq, k_mem, v_mem, w_xo, g_post, g_next, batch, seq, tm=512):
    m, d = x.shape
    n_mem = k_mem.shape[0] // batch
    per_b = seq // tm
    row = pl.BlockSpec((tm, d), lambda i: (i, 0))
    vec = pl.BlockSpec((1, d), lambda i: (0, 0))
    return pl.pallas_call(
        _xattn_kernel,
        grid=(m // tm,),
        in_specs=[row, vec,
                  pl.BlockSpec((d, XA_WIDTH), lambda i: (0, 0), pipeline_mode=pl.Buffered(1)),
                  pl.BlockSpec((n_mem, XA_WIDTH), lambda i: (i // per_b, 0)),
                  pl.BlockSpec((n_mem, XA_WIDTH), lambda i: (i // per_b, 0)),
                  pl.BlockSpec((XA_WIDTH, d), lambda i: (0, 0), pipeline_mode=pl.Buffered(1)),
                  vec, vec],
        out_specs=[row, row],
        out_shape=[jax.ShapeDtypeStruct((m, d), F32), jax.ShapeDtypeStruct((m, d), BF16)],
        compiler_params=_params(("parallel",), _XATTN_VMEM_LIMIT),
        name="cross_attn",
    )(x, g_pre.reshape(1, d), w_xq, k_mem, v_mem, w_xo, g_post.reshape(1, d), g_next.reshape(1, d))


def kernel(x, mem, positions, mix_pre_g, w_in, sgu_ln_g, sgu_ln_b, w_spatial, b_spatial, w_branch_a, w_branch_b, w_gate, b_gate, w_out, mix_post_g, xa_pre_g, mem_norm_g, w_xq, w_xk, w_xv, w_xo, xa_post_g, mlp_pre_g, w_up, w_down, mlp_post_g):
    batch, seq, d = x.shape
    n_mem = mem.shape[1]
    depth = w_in.shape[0]
    m = batch * seq
    x2 = x.reshape(m, d)
    mem2 = mem.reshape(batch * n_mem, d)
    c_tab, sa_tab, sb_tab = _rope_tables(positions)

    for l in range(depth):
        w_in_bf = w_in[l].astype(BF16)

        h = _rmsnorm(x2, mix_pre_g[l])
        uv, w_gate_bf, w_a_bf, w_b_bf = _inproj_uv(h, w_in_bf, (w_gate[l], w_branch_a[l], w_branch_b[l]))
        os, lses = [], []
        for g in range(N_GROUPS):
            qkv = _inproj_qkv(h, w_in_bf, c_tab, sa_tab, sb_tab, g, batch, seq)
            o, lse = _band_attention(qkv, g, batch, seq)
            os.append(o)
            lses.append(lse)
        y_a = _combine(os, lses)
        y_b = _sgu(uv, sgu_ln_g[l], sgu_ln_b[l], w_spatial[l], b_spatial[l])
        merged, w_up_bf, w_out_bf, w_xq_bf, w_xk_bf, w_xv_bf, w_xo_bf = _merge(
            h, y_a, y_b, w_gate_bf, b_gate[l], w_a_bf, w_b_bf,
            (w_up[l], w_out[l], w_xq[l], w_xk[l], w_xv[l], w_xo[l]))
        x2 = _matmul_norm_residual(merged, w_out_bf, x2, mix_post_g[l], "out_proj_norm")

        mn = _rmsnorm(mem2, mem_norm_g[l])
        k_mem, = _matmul(mn, w_xk_bf, BF16, name="mem_k")
        v_mem, = _matmul(mn, w_xv_bf, BF16, name="mem_v")
        x2, h = _xattn(x2, xa_pre_g[l], w_xq_bf, k_mem, v_mem, w_xo_bf, xa_post_g[l], mlp_pre_g[l], batch, seq)

        a, w_down_bf = _matmul(h, w_up_bf, BF16, act="relu2", name="mlp_up", cast_ws=(w_down[l],))
        x2 = _matmul_norm_residual(a, w_down_bf, x2, mlp_post_g[l], "mlp_down_norm")
    return x2.reshape(batch, seq, d)
```

```python
import functools
import math

import jax
import jax.numpy as jnp
from jax import lax
from jax.experimental import pallas as pl
from jax.experimental.pallas import tpu as pltpu

F32 = jnp.float32
BF16 = jnp.bfloat16

D_MODEL = 4096
HEAD_DIM = 128
ATTN_GROUPS = ((128, 1), (512, 4), (2048, 16))
N_GROUPS = len(ATTN_GROUPS)
HEADS_PER_GROUP = 8
ATTN_WIDTH = HEADS_PER_GROUP * HEAD_DIM
ATTN_QKV_WIDTH = N_GROUPS * ATTN_WIDTH
BAND = 128
SGU_WIDTH = D_MODEL // 2
SGU_GROUP_CH = 128
SGU_GROUPS = SGU_WIDTH // SGU_GROUP_CH
CHUNK = 128
IN_WIDTH = 3 * ATTN_QKV_WIDTH + 2 * SGU_WIDTH
ROPE_THETA = 500000.0
ROT_DIM = HEAD_DIM // 4
ROT_HALF = ROT_DIM // 2
XA_HEADS = 4
XA_WIDTH = XA_HEADS * HEAD_DIM
D_FF = 4 * D_MODEL
EPS = 1e-6
NEG = -1e30
ATTN_SCALE = HEAD_DIM ** -0.5
LANES = 128
BF16_SUBLANES = 16

VMEM_LIMIT = 56 * 1024 * 1024
RESIDENT_VMEM_LIMIT = 58 * 1024 * 1024


def _params(sem, vmem=VMEM_LIMIT):
    return pltpu.CompilerParams(dimension_semantics=sem, vmem_limit_bytes=vmem)


def _gelu_tanh(x):
    c = math.sqrt(2.0 / math.pi)
    return 0.5 * x * (1.0 + jnp.tanh(c * (x + 0.044715 * (x * x * x))))


def _rms(v, g):
    return v * lax.rsqrt(jnp.mean(v * v, axis=-1, keepdims=True) + EPS) * g


def _cast_spec(w, grid):
    rows, cols = w.shape
    nblk = math.prod(grid)
    while rows % nblk or (rows // nblk) % BF16_SUBLANES:
        nblk -= 1

    def imap(*idx):
        step = idx[0]
        for extent, ix in zip(grid[1:], idx[1:]):
            step = step * extent + ix
        return jnp.minimum(step, nblk - 1), 0

    return pl.BlockSpec((rows // nblk, cols), imap)


def _call_with_casts(body, cast_ws, *, grid, in_specs, out_specs, out_shape, args, **kw):
    n_in, n_out, n_cast = len(in_specs), len(out_specs), len(cast_ws)

    def kern(*refs):
        ins, refs = refs[:n_in], refs[n_in:]
        cast_in, refs = refs[:n_cast], refs[n_cast:]
        outs, refs = refs[:n_out], refs[n_out:]
        cast_out, scratch = refs[:n_cast], refs[n_cast:]
        for src, dst in zip(cast_in, cast_out):
            dst[...] = src[...].astype(dst.dtype)
        body(*ins, *outs, *scratch)

    cspecs = [_cast_spec(w, grid) for w in cast_ws]
    return pl.pallas_call(
        kern,
        grid=grid,
        in_specs=list(in_specs) + cspecs,
        out_specs=list(out_specs) + cspecs,
        out_shape=list(out_shape) + [jax.ShapeDtypeStruct(w.shape, BF16) for w in cast_ws],
        **kw,
    )(*args, *cast_ws)


def _rope_table_kernel(pos_ref, inv_ref, c_ref, sa_ref, sb_ref):
    ang = pos_ref[...] * inv_ref[...]
    c = jnp.cos(ang)
    s = jnp.sin(ang)
    lane = lax.broadcasted_iota(jnp.int32, ang.shape, 1)
    c_ref[...] = jnp.where(lane < ROT_DIM, c, 1.0)
    sa_ref[...] = jnp.where(lane < ROT_HALF, -s, 0.0)
    sb_ref[...] = jnp.where((lane >= ROT_HALF) & (lane < ROT_DIM), s, 0.0)


def _rope_tables(positions):
    m = positions.size
    tm = 2048
    inv = ROPE_THETA ** (-jnp.arange(0, ROT_DIM, 2, dtype=F32) / ROT_DIM)
    inv_lane = jnp.concatenate([inv, inv, jnp.zeros((LANES - ROT_DIM,), F32)])[None, :]
    pos = positions.astype(F32).reshape(m, 1)
    out = jax.ShapeDtypeStruct((m, LANES), F32)
    spec = pl.BlockSpec((tm, LANES), lambda i: (i, 0))
    return pl.pallas_call(
        _rope_table_kernel,
        grid=(m // tm,),
        in_specs=[pl.BlockSpec((tm, 1), lambda i: (i, 0)), pl.BlockSpec((1, LANES), lambda i: (0, 0))],
        out_specs=[spec, spec, spec],
        out_shape=[out, out, out],
        compiler_params=_params(("parallel",)),
        name="rope_tables",
    )(pos, inv_lane)


def _rmsnorm_kernel(x_ref, g_ref, o_ref):
    o_ref[...] = _rms(x_ref[...], g_ref[...]).astype(o_ref.dtype)


def _rmsnorm(x2d, g, out_dtype=BF16, tm=512):
    m, d = x2d.shape
    return pl.pallas_call(
        _rmsnorm_kernel,
        grid=(m // tm,),
        in_specs=[pl.BlockSpec((tm, d), lambda i: (i, 0)), pl.BlockSpec((1, d), lambda i: (0, 0))],
        out_specs=pl.BlockSpec((tm, d), lambda i: (i, 0)),
        out_shape=jax.ShapeDtypeStruct((m, d), out_dtype),
        compiler_params=_params(("parallel",)),
        name="rmsnorm",
    )(x2d, g.reshape(1, d))


_UV_TILES = 2 * SGU_WIDTH // ATTN_WIDTH
_MAX_FAST_STRIDE = 4
_DOT_CHUNK = 2 * HEAD_DIM


def _pipelined_tiles(dot_into, finish, acc_a, acc_b):
    s = pl.program_id(0)
    last = pl.num_programs(0) - 1

    @pl.when(s == 0)
    def _():
        acc_b[...] = jnp.zeros_like(acc_b)

    @pl.when((s % 2 == 0) & (s < last))
    def _():
        dot_into(acc_a)
        finish(acc_b)

    @pl.when(s % 2 == 1)
    def _():
        dot_into(acc_b)
        finish(acc_a)

    @pl.when(s == last)
    def _():
        finish(acc_b)


def _dot_into_heads(h_ref, w_ref, acc):
    h = h_ref[...]
    for cc in range(ATTN_WIDTH // _DOT_CHUNK):
        part = jnp.dot(h, w_ref[:, cc * _DOT_CHUNK:(cc + 1) * _DOT_CHUNK], preferred_element_type=F32)
        for hc in range(_DOT_CHUNK // HEAD_DIM):
            acc[cc * (_DOT_CHUNK // HEAD_DIM) + hc] = part[:, hc * HEAD_DIM:(hc + 1) * HEAD_DIM]


def _inproj_qkv_kernel(h_ref, w_ref, c_ref, sa_ref, sb_ref, o_ref, acc_a, acc_b, *scratch, dil):
    tm = h_ref.shape[0]

    def finish(acc):
        kind = jnp.maximum(pl.program_id(0) - 1, 0) % 3
        c = c_ref[...]
        sa = sa_ref[...]
        sb = sb_ref[...]
        rotate = kind < 2
        scale = jnp.where(kind == 0, ATTN_SCALE, 1.0).astype(F32)
        for hh in range(HEADS_PER_GROUP):
            osl = slice(hh * HEAD_DIM, (hh + 1) * HEAD_DIM)
            xh = acc[hh]
            y = xh * c + pltpu.roll(xh, LANES - ROT_HALF, 1) * sa + pltpu.roll(xh, ROT_HALF, 1) * sb
            y = jnp.where(rotate, y * scale, xh)
            if dil == 1:
                o_ref[0, :, osl] = y.astype(o_ref.dtype)
                continue
            acc[hh] = y
            if dil <= _MAX_FAST_STRIDE:
                for r in range(dil):
                    rows = acc[hh, pl.ds(r, tm // dil, stride=dil), :]
                    o_ref[r, :, osl] = rows.astype(o_ref.dtype)
            else:
                outer = dil // _MAX_FAST_STRIDE
                for r_lo in range(_MAX_FAST_STRIDE):
                    scratch[0][hh] = acc[hh, pl.ds(r_lo, tm // _MAX_FAST_STRIDE, stride=_MAX_FAST_STRIDE), :]
                    for r_hi in range(outer):
                        rows = scratch[0][hh, pl.ds(r_hi, tm // dil, stride=outer), :]
                        o_ref[r_hi * _MAX_FAST_STRIDE + r_lo, :, osl] = rows.astype(o_ref.dtype)

    _pipelined_tiles(functools.partial(_dot_into_heads, h_ref, w_ref), finish, acc_a, acc_b)


def _inproj_uv_kernel(h_ref, w_ref, o_ref):
    h = h_ref[...]
    for cc in range(ATTN_WIDTH // _DOT_CHUNK):
        sl = slice(cc * _DOT_CHUNK, (cc + 1) * _DOT_CHUNK)
        acc = jnp.dot(h, w_ref[:, sl], preferred_element_type=F32)
        o_ref[:, sl] = _gelu_tanh(acc).astype(o_ref.dtype)


def _tile_maps(n_tiles):
    assert n_tiles % 2 == 0
    return (lambda s: jnp.minimum(s, n_tiles - 1)), (lambda s: jnp.maximum(s - 1, 0))


def _inproj_qkv(h, w_in_bf16, c_tab, sa_tab, sb_tab, g, batch, seq, tm=1024):
    m, k = h.shape
    tn = ATTN_WIDTH
    dil = ATTN_GROUPS[g][1]
    per_b = seq // tm
    n_tiles = (m // tm) * 3
    dot_tile, fin_tile = _tile_maps(n_tiles)

    def out_map(s):
        i, kind = fin_tile(s) // 3, fin_tile(s) % 3
        return i // per_b, 0, i % per_b, kind

    tab_spec = pl.BlockSpec((tm, LANES), lambda s: (fin_tile(s) // 3, 0))
    acc = pltpu.VMEM((HEADS_PER_GROUP, tm, HEAD_DIM), F32)
    scratch = [acc, acc]
    if dil > _MAX_FAST_STRIDE:
        scratch.append(pltpu.VMEM((HEADS_PER_GROUP, tm // _MAX_FAST_STRIDE, HEAD_DIM), F32))
    return pl.pallas_call(
        functools.partial(_inproj_qkv_kernel, dil=dil),
        grid=(n_tiles + 1,),
        in_specs=[pl.BlockSpec((tm, k), lambda s: (dot_tile(s) // 3, 0)),
                  pl.BlockSpec((k, tn), lambda s: (0, (dot_tile(s) % 3) * N_GROUPS + g)),
                  tab_spec, tab_spec, tab_spec],
        out_specs=pl.BlockSpec((None, dil, tm // dil, tn), out_map),
        out_shape=jax.ShapeDtypeStruct((batch, dil, seq // dil, 3 * tn), BF16),
        scratch_shapes=scratch,
        compiler_params=_params(("arbitrary",)),
        name=f"inproj_qkv_g{g}",
    )(h, w_in_bf16, c_tab, sa_tab, sb_tab)


def _inproj_uv(h, w_in_bf16, cast_ws, tm=1024):
    m, k = h.shape
    tn = ATTN_WIDTH
    first = 3 * ATTN_QKV_WIDTH // tn
    return _call_with_casts(
        _inproj_uv_kernel, cast_ws,
        grid=(m // tm, _UV_TILES),
        in_specs=[pl.BlockSpec((tm, k), lambda i, j: (i, 0)),
                  pl.BlockSpec((k, tn), lambda i, j: (0, first + j))],
        out_specs=[pl.BlockSpec((tm, tn), lambda i, j: (i, j))],
        out_shape=[jax.ShapeDtypeStruct((m, _UV_TILES * tn), BF16)],
        args=(h, w_in_bf16),
        compiler_params=_params(("parallel", "arbitrary")),
        name="inproj_uv",
    )


_ATTN_QBLOCKS = 2


def _band_attn_kernel(q_ref, kp_ref, kc_ref, vp_ref, vc_ref, o_ref, lse_ref, *scratch, dil):
    i = pl.program_id(1)
    r = pl.program_id(2)
    row = lax.broadcasted_iota(jnp.int32, (BAND, BAND), 0)
    col = lax.broadcasted_iota(jnp.int32, (BAND, BAND), 1)
    mask_cur = col <= row
    mask_band = col >= row

    def heads(ref, blk):
        rows = slice(blk * BAND, (blk + 1) * BAND)
        return jnp.stack([ref[rows, hh * HEAD_DIM:(hh + 1) * HEAD_DIM] for hh in range(HEADS_PER_GROUP)])

    for blk in range(_ATTN_QBLOCKS):
        q, k_c, v_c = heads(q_ref, blk), heads(kc_ref, blk), heads(vc_ref, blk)
        if blk == 0:
            k_p, v_p = heads(kp_ref, 0), heads(vp_ref, 0)
            mask_prev = mask_band & (i > 0)
        else:
            k_p, v_p = heads(kc_ref, blk - 1), heads(vc_ref, blk - 1)
            mask_prev = mask_band
        s_c = jnp.einsum("hqd,hkd->hqk", q, k_c, preferred_element_type=F32)
        s_p = jnp.einsum("hqd,hkd->hqk", q, k_p, preferred_element_type=F32)
        s_c = jnp.where(mask_cur, s_c, NEG)
        s_p = jnp.where(mask_prev, s_p, NEG)
        mx = jnp.maximum(jnp.max(s_c, axis=-1, keepdims=True), jnp.max(s_p, axis=-1, keepdims=True))
        p_c = jnp.exp(s_c - mx)
        p_p = jnp.exp(s_p - mx)
        den = jnp.sum(p_c, axis=-1, keepdims=True) + jnp.sum(p_p, axis=-1, keepdims=True)
        o = jnp.einsum("hqk,hkd->hqd", p_c.astype(BF16), v_c, preferred_element_type=F32)
        o = o + jnp.einsum("hqk,hkd->hqd", p_p.astype(BF16), v_p, preferred_element_type=F32)
        o = o / den
        lse = mx + jnp.log(den)
        lse_tile = jnp.zeros((BAND, LANES), F32)
        rows = slice(blk * BAND, (blk + 1) * BAND) if dil == 1 else pl.ds(blk * BAND * dil + r, BAND, stride=dil)
        for hh in range(HEADS_PER_GROUP):
            if dil == 1:
                o_ref[rows, hh * HEAD_DIM:(hh + 1) * HEAD_DIM] = o[hh].astype(o_ref.dtype)
            else:
                scratch[0][hh, rows, :] = o[hh]
            lse_tile = jnp.where(col == hh, lse[hh], lse_tile)
        lse_ref[rows, :] = lse_tile

    if dil > 1:
        @pl.when(r == dil - 1)
        def _():
            for hh in range(HEADS_PER_GROUP):
                o_ref[:, hh * HEAD_DIM:(hh + 1) * HEAD_DIM] = scratch[0][hh].astype(o_ref.dtype)


def _band_attention(qkv, g, batch, seq):
    _, dil = ATTN_GROUPS[g]
    sub = seq // dil
    qrows = _ATTN_QBLOCKS * BAND
    nstep = sub // qrows
    rows = qrows * dil

    def cur(col):
        return pl.BlockSpec((None, None, qrows, ATTN_WIDTH), lambda b, i, r: (b, r, i, col))

    def prev(col):
        return pl.BlockSpec((None, None, BAND, ATTN_WIDTH),
                            lambda b, i, r: (b, r, jnp.maximum(i * _ATTN_QBLOCKS - 1, 0), col))

    return pl.pallas_call(
        functools.partial(_band_attn_kernel, dil=dil),
        grid=(batch, nstep, dil),
        in_specs=[cur(0), prev(1), cur(1), prev(2), cur(2)],
        out_specs=[pl.BlockSpec((rows, ATTN_WIDTH), lambda b, i, r: (b * nstep + i, 0)),
                   pl.BlockSpec((rows, LANES), lambda b, i, r: (b * nstep + i, 0))],
        out_shape=[jax.ShapeDtypeStruct((batch * seq, ATTN_WIDTH), BF16),
                   jax.ShapeDtypeStruct((batch * seq, LANES), F32)],
        scratch_shapes=[pltpu.VMEM((HEADS_PER_GROUP, rows, HEAD_DIM), F32)] if dil > 1 else [],
        compiler_params=_params(("parallel", "parallel", "arbitrary")),
        name=f"band_attn_g{g}",
    )(qkv, qkv, qkv, qkv, qkv)


def _combine_kernel(o0_ref, o1_ref, o2_ref, l0_ref, l1_ref, l2_ref, y_ref):
    l0, l1, l2 = l0_ref[...], l1_ref[...], l2_ref[...]
    mx = jnp.maximum(jnp.maximum(l0, l1), l2)
    e0, e1, e2 = jnp.exp(l0 - mx), jnp.exp(l1 - mx), jnp.exp(l2 - mx)
    den = e0 + e1 + e2
    a0, a1 = e0 / den, e1 / den
    rows = l0.shape[0]
    for hh in range(HEADS_PER_GROUP):
        sl = slice(hh * HEAD_DIM, (hh + 1) * HEAD_DIM)
        b0 = jnp.broadcast_to(a0[:, hh:hh + 1], (rows, HEAD_DIM))
        b1 = jnp.broadcast_to(a1[:, hh:hh + 1], (rows, HEAD_DIM))
        o2 = o2_ref[:, sl].astype(F32)
        y = o2 + b0 * (o0_ref[:, sl].astype(F32) - o2) + b1 * (o1_ref[:, sl].astype(F32) - o2)
        y_ref[:, sl] = y.astype(y_ref.dtype)


def _combine(os, lses, tm=512):
    m = os[0].shape[0]
    ospec = pl.BlockSpec((tm, ATTN_WIDTH), lambda i: (i, 0))
    lspec = pl.BlockSpec((tm, LANES), lambda i: (i, 0))
    return pl.pallas_call(
        _combine_kernel,
        grid=(m // tm,),
        in_specs=[ospec, ospec, ospec, lspec, lspec, lspec],
        out_specs=ospec,
        out_shape=jax.ShapeDtypeStruct((m, ATTN_WIDTH), BF16),
        compiler_params=_params(("parallel",)),
        name="attn_combine",
    )(*os, *lses)


_SGU_ROWS = 4 * CHUNK


def _sgu_kernel(u0_ref, u1_ref, v0_ref, v1_ref, g_ref, b_ref, ws_ref, bt_ref, o_ref):
    half = SGU_WIDTH // 2
    nch = v0_ref.shape[0] // CHUNK
    v0 = v0_ref[...].astype(F32)
    v1 = v1_ref[...].astype(F32)
    mu = (jnp.sum(v0, axis=-1, keepdims=True) + jnp.sum(v1, axis=-1, keepdims=True)) / SGU_WIDTH
    d0, d1 = v0 - mu, v1 - mu
    var = (jnp.sum(d0 * d0, axis=-1, keepdims=True) + jnp.sum(d1 * d1, axis=-1, keepdims=True)) / SGU_WIDTH
    inv = lax.rsqrt(var + EPS)
    vn = ((d0 * inv * g_ref[:, :half] + b_ref[:, :half]).astype(BF16),
          (d1 * inv * g_ref[:, half:] + b_ref[:, half:]).astype(BF16))
    u_refs = (u0_ref, u1_ref)
    row = lax.broadcasted_iota(jnp.int32, (CHUNK, CHUNK), 0)
    col = lax.broadcasted_iota(jnp.int32, (CHUNK, CHUNK), 1)
    tri = col <= row
    per_half = SGU_GROUPS // 2
    for gg in range(SGU_GROUPS):
        hf, gl = divmod(gg, per_half)
        sl = slice(gl * SGU_GROUP_CH, (gl + 1) * SGU_GROUP_CH)
        osl = slice(gg * SGU_GROUP_CH, (gg + 1) * SGU_GROUP_CH)
        w = jnp.where(tri, ws_ref[gg], 0.0).astype(BF16)
        v_all = jnp.concatenate([vn[hf][c * CHUNK:(c + 1) * CHUNK, sl] for c in range(nch)], axis=1)
        mixed = jnp.dot(w, v_all, preferred_element_type=F32)
        bias = bt_ref[:, gg:gg + 1]
        for c in range(nch):
            rs = slice(c * CHUNK, (c + 1) * CHUNK)
            m_c = mixed[:, c * SGU_GROUP_CH:(c + 1) * SGU_GROUP_CH] + bias
            o_ref[rs, osl] = (u_refs[hf][rs, sl].astype(F32) * m_c).astype(o_ref.dtype)


def _sgu(proj, ln_g, ln_b, w_spatial, b_spatial):
    m = proj.shape[0]
    half = SGU_WIDTH // 2
    vec = pl.BlockSpec((1, SGU_WIDTH), lambda i: (0, 0))

    def col_spec(c):
        return pl.BlockSpec((_SGU_ROWS, half), lambda i: (i, c))

    return pl.pallas_call(
        _sgu_kernel,
        grid=(m // _SGU_ROWS,),
        in_specs=[col_spec(0), col_spec(1), col_spec(2), col_spec(3),
                  vec, vec,
                  pl.BlockSpec((SGU_GROUPS, CHUNK, CHUNK), lambda i: (0, 0, 0)),
                  pl.BlockSpec((CHUNK, SGU_GROUPS), lambda i: (0, 0))],
        out_specs=pl.BlockSpec((_SGU_ROWS, SGU_WIDTH), lambda i: (i, 0)),
        out_shape=jax.ShapeDtypeStruct((m, SGU_WIDTH), BF16),
        compiler_params=_params(("parallel",)),
        name="sgu",
    )(proj, proj, proj, proj, ln_g.reshape(1, -1), ln_b.reshape(1, -1), w_spatial, b_spatial.T)


def _merge_kernel(h_ref, ya_ref, yb_ref, wga_ref, wgb_ref, wa_ref, wb_ref, bga_ref, bgb_ref, o_ref):
    h = h_ref[...]
    ga = jax.nn.sigmoid(jnp.dot(h, wga_ref[...], preferred_element_type=F32) + bga_ref[...])
    a = jnp.dot(ya_ref[...], wa_ref[...], preferred_element_type=F32)
    acc = ga * a
    gb = jax.nn.sigmoid(jnp.dot(h, wgb_ref[...], preferred_element_type=F32) + bgb_ref[...])
    b = jnp.dot(yb_ref[...], wb_ref[...], preferred_element_type=F32)
    o_ref[...] = (acc + gb * b).astype(o_ref.dtype)


def _merge(h, ya, yb, w_gate, b_gate, w_a, w_b, cast_ws, tm=512, tn=512):
    m = h.shape[0]
    nb = D_MODEL // tn
    b_gate = b_gate.reshape(1, -1)
    return _call_with_casts(
        _merge_kernel, cast_ws,
        grid=(m // tm, nb),
        in_specs=[pl.BlockSpec((tm, D_MODEL), lambda i, j: (i, 0)),
                  pl.BlockSpec((tm, ATTN_WIDTH), lambda i, j: (i, 0)),
                  pl.BlockSpec((tm, SGU_WIDTH), lambda i, j: (i, 0)),
                  pl.BlockSpec((D_MODEL, tn), lambda i, j: (0, j)),
                  pl.BlockSpec((D_MODEL, tn), lambda i, j: (0, j + nb)),
                  pl.BlockSpec((ATTN_WIDTH, tn), lambda i, j: (0, j)),
                  pl.BlockSpec((SGU_WIDTH, tn), lambda i, j: (0, j)),
                  pl.BlockSpec((1, tn), lambda i, j: (0, j)),
                  pl.BlockSpec((1, tn), lambda i, j: (0, j + nb))],
        out_specs=[pl.BlockSpec((tm, tn), lambda i, j: (i, j))],
        out_shape=[jax.ShapeDtypeStruct((m, D_MODEL), BF16)],
        args=(h, ya, yb, w_gate, w_gate, w_a, w_b, b_gate, b_gate),
        compiler_params=_params(("parallel", "arbitrary")),
        name="gated_merge",
    )


def _matmul_kernel(a_ref, w_ref, o_ref, *, act):
    acc = jnp.dot(a_ref[...], w_ref[...], preferred_element_type=F32)
    if act == "relu2":
        r = jnp.maximum(acc, 0.0)
        acc = r * r
    o_ref[...] = acc.astype(o_ref.dtype)


def _matmul(a, w, out_dtype, act=None, tm=1024, tn=1024, name="matmul", cast_ws=()):
    m, k = a.shape
    n = w.shape[1]
    tm, tn = min(tm, m), min(tn, n)
    return _call_with_casts(
        functools.partial(_matmul_kernel, act=act), cast_ws,
        grid=(m // tm, n // tn),
        in_specs=[pl.BlockSpec((tm, k), lambda i, j: (i, 0)), pl.BlockSpec((k, tn), lambda i, j: (0, j))],
        out_specs=[pl.BlockSpec((tm, tn), lambda i, j: (i, j))],
        out_shape=[jax.ShapeDtypeStruct((m, n), out_dtype)],
        args=(a, w),
        compiler_params=_params(("parallel", "arbitrary")),
        name=name,
    )


_NORM_DOT_CHUNK = 1024


def _matmul_norm_kernel(a_ref, w_ref, x_ref, g_ref, xo_ref, xs_ref):
    k = pl.program_id(1)
    d = xo_ref.shape[1]
    xw = x_ref.shape[1]
    xs_ref[:, pl.ds(pl.multiple_of(k * xw, xw), xw)] = x_ref[...]
    a = a_ref[...]
    for c in range(d // _NORM_DOT_CHUNK):
        sl = slice(c * _NORM_DOT_CHUNK, (c + 1) * _NORM_DOT_CHUNK)
        part = jnp.dot(a, w_ref[:, sl], preferred_element_type=F32)
        xo_ref[:, sl] = jnp.where(k == 0, 0.0, xo_ref[:, sl]) + part

    @pl.when(k == pl.num_programs(1) - 1)
    def _():
        xo_ref[...] = xs_ref[...] + _rms(xo_ref[...], g_ref[...])


def _matmul_norm_residual(a, w, x, g, name, tm=512, tk=1024):
    m, kdim = a.shape
    d = w.shape[1]
    nk = kdim // tk
    return pl.pallas_call(
        _matmul_norm_kernel,
        grid=(m // tm, nk),
        in_specs=[pl.BlockSpec((tm, tk), lambda i, k: (i, k)),
                  pl.BlockSpec((tk, d), lambda i, k: (k, 0)),
                  pl.BlockSpec((tm, d // nk), lambda i, k: (i, k)),
                  pl.BlockSpec((1, d), lambda i, k: (0, 0))],
        out_specs=pl.BlockSpec((tm, d), lambda i, k: (i, 0)),
        out_shape=jax.ShapeDtypeStruct((m, d), F32),
        scratch_shapes=[pltpu.VMEM((tm, d), F32)],
        compiler_params=_params(("parallel", "arbitrary")),
        name=name,
    )(a, w, x, g.reshape(1, d))


def _resident_norm_kernel(a_ref, w_ref, x_ref, g_ref, xo_ref):
    a = a_ref[...]
    d = xo_ref.shape[1]
    parts = [jnp.dot(a, w_ref[:, c * _NORM_DOT_CHUNK:(c + 1) * _NORM_DOT_CHUNK], preferred_element_type=F32)
             for c in range(d // _NORM_DOT_CHUNK)]
    y = jnp.concatenate(parts, axis=-1)
    xo_ref[...] = x_ref[...] + _rms(y, g_ref[...])


def _resident_norm_residual(a, w, x, g, name, tm=256):
    m, k = a.shape
    d = w.shape[1]
    row = lambda width: pl.BlockSpec((tm, width), lambda i: (i, 0))
    return pl.pallas_call(
        _resident_norm_kernel,
        grid=(m // tm,),
        in_specs=[row(k),
                  pl.BlockSpec((k, d), lambda i: (0, 0), pipeline_mode=pl.Buffered(1)),
                  row(d),
                  pl.BlockSpec((1, d), lambda i: (0, 0))],
        out_specs=row(d),
        out_shape=jax.ShapeDtypeStruct((m, d), F32),
        compiler_params=_params(("arbitrary",), RESIDENT_VMEM_LIMIT),
        name=name,
    )(a, w, x, g.reshape(1, d))


def _xattn_kernel(x_ref, gpre_ref, wq_ref, k_ref, v_ref, wo_ref, gpost_ref, gnext_ref, xo_ref, hn_ref):
    x = x_ref[...]
    h = _rms(x, gpre_ref[...]).astype(BF16)
    q = jnp.dot(h, wq_ref[...], preferred_element_type=F32) * ATTN_SCALE
    q = q.astype(BF16)
    dn = (((1,), (1,)), ((), ()))
    outs = []
    for hh in range(XA_HEADS):
        sl = slice(hh * HEAD_DIM, (hh + 1) * HEAD_DIM)
        s = lax.dot_general(q[:, sl], k_ref[:, sl], dn, preferred_element_type=F32)
        mx = jnp.max(s, axis=-1, keepdims=True)
        p = jnp.exp(s - mx)
        den = jnp.sum(p, axis=-1, keepdims=True)
        o = jnp.dot(p.astype(BF16), v_ref[:, sl], preferred_element_type=F32)
        outs.append((o / den).astype(BF16))
    o_all = jnp.concatenate(outs, axis=-1)
    y = jnp.dot(o_all, wo_ref[...], preferred_element_type=F32)
    x_new = x + _rms(y, gpost_ref[...])
    xo_ref[...] = x_new
    hn_ref[...] = _rms(x_new, gnext_ref[...]).astype(hn_ref.dtype)


def _xattn(x, g_pre, w_xq, k_mem, v_mem, w_xo, g_post, g_next, batch, seq, tm=512):
    m, d = x.shape
    n_mem = k_mem.shape[0] // batch
    per_b = seq // tm
    row = pl.BlockSpec((tm, d), lambda i: (i, 0))
    vec = pl.BlockSpec((1, d), lambda i: (0, 0))
    return pl.pallas_call(
        _xattn_kernel,
        grid=(m // tm,),
        in_specs=[row, vec,
                  pl.BlockSpec((d, XA_WIDTH), lambda i: (0, 0), pipeline_mode=pl.Buffered(1)),
                  pl.BlockSpec((n_mem, XA_WIDTH), lambda i: (i // per_b, 0)),
                  pl.BlockSpec((n_mem, XA_WIDTH), lambda i: (i // per_b, 0)),
                  pl.BlockSpec((XA_WIDTH, d), lambda i: (0, 0), pipeline_mode=pl.Buffered(1)),
                  vec, vec],
        out_specs=[row, row],
        out_shape=[jax.ShapeDtypeStruct((m, d), F32), jax.ShapeDtypeStruct((m, d), BF16)],
        compiler_params=_params(("parallel",), RESIDENT_VMEM_LIMIT),
        name="cross_attn",
    )(x, g_pre.reshape(1, d), w_xq, k_mem, v_mem, w_xo, g_post.reshape(1, d), g_next.reshape(1, d))


def kernel(x, mem, positions, mix_pre_g, w_in, sgu_ln_g, sgu_ln_b, w_spatial, b_spatial, w_branch_a, w_branch_b, w_gate, b_gate, w_out, mix_post_g, xa_pre_g, mem_norm_g, w_xq, w_xk, w_xv, w_xo, xa_post_g, mlp_pre_g, w_up, w_down, mlp_post_g):
    batch, seq, d = x.shape
    n_mem = mem.shape[1]
    depth = w_in.shape[0]
    m = batch * seq
    x2 = x.reshape(m, d)
    mem2 = mem.reshape(batch * n_mem, d)
    c_tab, sa_tab, sb_tab = _rope_tables(positions)

    for l in range(depth):
        w_in_bf = w_in[l].astype(BF16)

        h = _rmsnorm(x2, mix_pre_g[l])
        uv, w_gate_bf, w_a_bf, w_b_bf = _inproj_uv(h, w_in_bf, (w_gate[l], w_branch_a[l], w_branch_b[l]))
        os, lses = [], []
        for g in range(N_GROUPS):
            qkv = _inproj_qkv(h, w_in_bf, c_tab, sa_tab, sb_tab, g, batch, seq)
            o, lse = _band_attention(qkv, g, batch, seq)
            os.append(o)
            lses.append(lse)
        y_a = _combine(os, lses)
        y_b = _sgu(uv, sgu_ln_g[l], sgu_ln_b[l], w_spatial[l], b_spatial[l])
        merged, w_up_bf, w_out_bf, w_xq_bf, w_xk_bf, w_xv_bf, w_xo_bf = _merge(
            h, y_a, y_b, w_gate_bf, b_gate[l], w_a_bf, w_b_bf,
            (w_up[l], w_out[l], w_xq[l], w_xk[l], w_xv[l], w_xo[l]))
        x2 = _resident_norm_residual(merged, w_out_bf, x2, mix_post_g[l], "out_proj_norm")

        mn = _rmsnorm(mem2, mem_norm_g[l])
        k_mem, = _matmul(mn, w_xk_bf, BF16, name="mem_k")
        v_mem, = _matmul(mn, w_xv_bf, BF16, name="mem_v")
        x2, h = _xattn(x2, xa_pre_g[l], w_xq_bf, k_mem, v_mem, w_xo_bf, xa_post_g[l], mlp_pre_g[l], batch, seq)

        a, w_down_bf = _matmul(h, w_up_bf, BF16, act="relu2", name="mlp_up", cast_ws=(w_down[l],))
        x2 = _matmul_norm_residual(a, w_down_bf, x2, mlp_post_g[l], "mlp_down_norm")
    return x2.reshape(batch, seq, d)
```

```python
import functools
import math

import jax
import jax.numpy as jnp
from jax import lax
from jax.experimental import pallas as pl
from jax.experimental.pallas import tpu as pltpu

F32 = jnp.float32
BF16 = jnp.bfloat16

D_MODEL = 4096
HEAD_DIM = 128
ATTN_GROUPS = ((128, 1), (512, 4), (2048, 16))
N_GROUPS = len(ATTN_GROUPS)
HEADS_PER_GROUP = 8
ATTN_WIDTH = HEADS_PER_GROUP * HEAD_DIM
ATTN_QKV_WIDTH = N_GROUPS * ATTN_WIDTH
BAND = 128
SGU_WIDTH = D_MODEL // 2
SGU_GROUP_CH = 128
SGU_GROUPS = SGU_WIDTH // SGU_GROUP_CH
CHUNK = 128
IN_WIDTH = 3 * ATTN_QKV_WIDTH + 2 * SGU_WIDTH
ROPE_THETA = 500000.0
ROT_DIM = HEAD_DIM // 4
ROT_HALF = ROT_DIM // 2
XA_HEADS = 4
XA_WIDTH = XA_HEADS * HEAD_DIM
D_FF = 4 * D_MODEL
EPS = 1e-6
NEG = -1e30
ATTN_SCALE = HEAD_DIM ** -0.5
LANES = 128
BF16_SUBLANES = 16

VMEM_LIMIT = 56 * 1024 * 1024
RESIDENT_VMEM_LIMIT = 58 * 1024 * 1024


def _params(sem, vmem=VMEM_LIMIT):
    return pltpu.CompilerParams(dimension_semantics=sem, vmem_limit_bytes=vmem)


def _gelu_tanh(x):
    c = math.sqrt(2.0 / math.pi)
    return 0.5 * x * (1.0 + jnp.tanh(c * (x + 0.044715 * (x * x * x))))


def _rms(v, g):
    return v * lax.rsqrt(jnp.mean(v * v, axis=-1, keepdims=True) + EPS) * g


def _cast_spec(w, grid):
    rows, cols = w.shape
    nblk = math.prod(grid)
    while rows % nblk or (rows // nblk) % BF16_SUBLANES:
        nblk -= 1

    def imap(*idx):
        step = idx[0]
        for extent, ix in zip(grid[1:], idx[1:]):
            step = step * extent + ix
        return jnp.minimum(step, nblk - 1), 0

    return pl.BlockSpec((rows // nblk, cols), imap)


def _call_with_casts(body, cast_ws, *, grid, in_specs, out_specs, out_shape, args, **kw):
    n_in, n_out, n_cast = len(in_specs), len(out_specs), len(cast_ws)

    def kern(*refs):
        ins, refs = refs[:n_in], refs[n_in:]
        cast_in, refs = refs[:n_cast], refs[n_cast:]
        outs, refs = refs[:n_out], refs[n_out:]
        cast_out, scratch = refs[:n_cast], refs[n_cast:]
        for src, dst in zip(cast_in, cast_out):
            dst[...] = src[...].astype(dst.dtype)
        body(*ins, *outs, *scratch)

    cspecs = [_cast_spec(w, grid) for w in cast_ws]
    return pl.pallas_call(
        kern,
        grid=grid,
        in_specs=list(in_specs) + cspecs,
        out_specs=list(out_specs) + cspecs,
        out_shape=list(out_shape) + [jax.ShapeDtypeStruct(w.shape, BF16) for w in cast_ws],
        **kw,
    )(*args, *cast_ws)


def _rope_table_rows(pos_ref, inv_ref, c_ref, sa_ref, sb_ref):
    ang = pos_ref[...] * inv_ref[...]
    c = jnp.cos(ang)
    s = jnp.sin(ang)
    lane = lax.broadcasted_iota(jnp.int32, ang.shape, 1)
    c_ref[...] = jnp.where(lane < ROT_DIM, c, 1.0)
    sa_ref[...] = jnp.where(lane < ROT_HALF, -s, 0.0)
    sb_ref[...] = jnp.where((lane >= ROT_HALF) & (lane < ROT_DIM), s, 0.0)


def _rmsnorm_kernel(x_ref, g_ref, o_ref):
    o_ref[...] = _rms(x_ref[...], g_ref[...]).astype(o_ref.dtype)


def _rmsnorm_rope_kernel(x_ref, g_ref, pos_ref, inv_ref, o_ref, c_ref, sa_ref, sb_ref):
    _rmsnorm_kernel(x_ref, g_ref, o_ref)
    _rope_table_rows(pos_ref, inv_ref, c_ref, sa_ref, sb_ref)


def _rmsnorm(x2d, g, positions=None, tm=512):
    m, d = x2d.shape
    row = pl.BlockSpec((tm, d), lambda i: (i, 0))
    in_specs = [row, pl.BlockSpec((1, d), lambda i: (0, 0))]
    args = [x2d, g.reshape(1, d)]
    out_specs, out_shape = row, jax.ShapeDtypeStruct((m, d), BF16)
    kern = _rmsnorm_kernel
    if positions is not None:
        inv = ROPE_THETA ** (-jnp.arange(0, ROT_DIM, 2, dtype=F32) / ROT_DIM)
        inv_lane = jnp.concatenate([inv, inv, jnp.zeros((LANES - ROT_DIM,), F32)])[None, :]
        tab = pl.BlockSpec((tm, LANES), lambda i: (i, 0))
        in_specs += [pl.BlockSpec((tm, 1), lambda i: (i, 0)), pl.BlockSpec((1, LANES), lambda i: (0, 0))]
        args += [positions.astype(F32).reshape(m, 1), inv_lane]
        out_specs = [row, tab, tab, tab]
        out_shape = [out_shape] + [jax.ShapeDtypeStruct((m, LANES), F32)] * 3
        kern = _rmsnorm_rope_kernel
    return pl.pallas_call(
        kern,
        grid=(m // tm,),
        in_specs=in_specs,
        out_specs=out_specs,
        out_shape=out_shape,
        compiler_params=_params(("parallel",)),
        name="rmsnorm",
    )(*args)


_UV_TILES = 2 * SGU_WIDTH // ATTN_WIDTH
_QKV_KINDS = 3
_MAX_FAST_STRIDE = 4
_DOT_CHUNK = 2 * HEAD_DIM


def _pipelined_tiles(dot_into, finish, acc_a, acc_b):
    s = pl.program_id(0)
    last = pl.num_programs(0) - 1

    @pl.when(s == 0)
    def _():
        acc_b[...] = jnp.zeros_like(acc_b)

    @pl.when((s % 2 == 0) & (s < last))
    def _():
        dot_into(acc_a)
        finish(acc_b)

    @pl.when(s % 2 == 1)
    def _():
        dot_into(acc_b)
        finish(acc_a)

    @pl.when(s == last)
    def _():
        finish(acc_b)


def _dot_into_heads(h_ref, w_ref, acc):
    h = h_ref[...]
    for cc in range(ATTN_WIDTH // _DOT_CHUNK):
        part = jnp.dot(h, w_ref[:, cc * _DOT_CHUNK:(cc + 1) * _DOT_CHUNK], preferred_element_type=F32)
        for hc in range(_DOT_CHUNK // HEAD_DIM):
            acc[cc * (_DOT_CHUNK // HEAD_DIM) + hc] = part[:, hc * HEAD_DIM:(hc + 1) * HEAD_DIM]


def _inproj_qkv_kernel(h_ref, w_ref, c_ref, sa_ref, sb_ref, o_ref, acc_a, acc_b, *scratch, dil):
    tm = h_ref.shape[0]

    def finish(acc):
        kind = jnp.maximum(pl.program_id(0) - 1, 0) % _QKV_KINDS
        c = c_ref[...]
        sa = sa_ref[...]
        sb = sb_ref[...]
        rotate = kind < 2
        scale = jnp.where(kind == 0, ATTN_SCALE, 1.0).astype(F32)
        for hh in range(HEADS_PER_GROUP):
            osl = slice(hh * HEAD_DIM, (hh + 1) * HEAD_DIM)
            xh = acc[hh]
            y = xh * c + pltpu.roll(xh, LANES - ROT_HALF, 1) * sa + pltpu.roll(xh, ROT_HALF, 1) * sb
            y = jnp.where(rotate, y * scale, xh)
            if dil == 1:
                o_ref[0, :, osl] = y.astype(o_ref.dtype)
                continue
            acc[hh] = y
            if dil <= _MAX_FAST_STRIDE:
                for r in range(dil):
                    rows = acc[hh, pl.ds(r, tm // dil, stride=dil), :]
                    o_ref[r, :, osl] = rows.astype(o_ref.dtype)
            else:
                outer = dil // _MAX_FAST_STRIDE
                for r_lo in range(_MAX_FAST_STRIDE):
                    scratch[0][hh] = acc[hh, pl.ds(r_lo, tm // _MAX_FAST_STRIDE, stride=_MAX_FAST_STRIDE), :]
                    for r_hi in range(outer):
                        rows = scratch[0][hh, pl.ds(r_hi, tm // dil, stride=outer), :]
                        o_ref[r_hi * _MAX_FAST_STRIDE + r_lo, :, osl] = rows.astype(o_ref.dtype)

    _pipelined_tiles(functools.partial(_dot_into_heads, h_ref, w_ref), finish, acc_a, acc_b)


def _inproj_uv_kernel(h_ref, w_ref, o_ref):
    h = h_ref[...]
    for cc in range(ATTN_WIDTH // _DOT_CHUNK):
        sl = slice(cc * _DOT_CHUNK, (cc + 1) * _DOT_CHUNK)
        acc = jnp.dot(h, w_ref[:, sl], preferred_element_type=F32)
        o_ref[:, sl] = _gelu_tanh(acc).astype(o_ref.dtype)


def _tile_maps(n_tiles):
    assert n_tiles % 2 == 0
    return (lambda s: jnp.minimum(s, n_tiles - 1)), (lambda s: jnp.maximum(s - 1, 0))


def _inproj_qkv(h, w_in_bf16, c_tab, sa_tab, sb_tab, g, batch, seq, tm=1024):
    m, k = h.shape
    tn = ATTN_WIDTH
    dil = ATTN_GROUPS[g][1]
    per_b = seq // tm
    n_tiles = (m // tm) * _QKV_KINDS
    dot_tile, fin_tile = _tile_maps(n_tiles)

    def out_map(s):
        i, kind = fin_tile(s) // _QKV_KINDS, fin_tile(s) % _QKV_KINDS
        return i // per_b, 0, i % per_b, kind

    tab_spec = pl.BlockSpec((tm, LANES), lambda s: (fin_tile(s) // _QKV_KINDS, 0))
    acc = pltpu.VMEM((HEADS_PER_GROUP, tm, HEAD_DIM), F32)
    scratch = [acc, acc]
    if dil > _MAX_FAST_STRIDE:
        scratch.append(pltpu.VMEM((HEADS_PER_GROUP, tm // _MAX_FAST_STRIDE, HEAD_DIM), F32))
    return pl.pallas_call(
        functools.partial(_inproj_qkv_kernel, dil=dil),
        grid=(n_tiles + 1,),
        in_specs=[pl.BlockSpec((tm, k), lambda s: (dot_tile(s) // _QKV_KINDS, 0)),
                  pl.BlockSpec((k, tn), lambda s: (0, (dot_tile(s) % _QKV_KINDS) * N_GROUPS + g)),
                  tab_spec, tab_spec, tab_spec],
        out_specs=pl.BlockSpec((None, dil, tm // dil, tn), out_map),
        out_shape=jax.ShapeDtypeStruct((batch, dil, seq // dil, _QKV_KINDS * tn), BF16),
        scratch_shapes=scratch,
        compiler_params=_params(("arbitrary",)),
        name=f"inproj_qkv_g{g}",
    )(h, w_in_bf16, c_tab, sa_tab, sb_tab)


def _inproj_uv(h, w_in_bf16, cast_ws, tm=1024):
    m, k = h.shape
    tn = ATTN_WIDTH
    first = _QKV_KINDS * ATTN_QKV_WIDTH // tn
    return _call_with_casts(
        _inproj_uv_kernel, cast_ws,
        grid=(m // tm, _UV_TILES),
        in_specs=[pl.BlockSpec((tm, k), lambda i, j: (i, 0)),
                  pl.BlockSpec((k, tn), lambda i, j: (0, first + j))],
        out_specs=[pl.BlockSpec((tm, tn), lambda i, j: (i, j))],
        out_shape=[jax.ShapeDtypeStruct((m, _UV_TILES * tn), BF16)],
        args=(h, w_in_bf16),
        compiler_params=_params(("parallel", "arbitrary")),
        name="inproj_uv",
    )


_ATTN_QBLOCKS = 2


def _band_attn_kernel(q_ref, kp_ref, kc_ref, vp_ref, vc_ref, o_ref, lse_ref, *scratch, dil):
    i = pl.program_id(1)
    r = pl.program_id(2)
    row = lax.broadcasted_iota(jnp.int32, (BAND, BAND), 0)
    col = lax.broadcasted_iota(jnp.int32, (BAND, BAND), 1)
    mask_cur = col <= row
    mask_band = col >= row

    def heads(ref, blk):
        rows = slice(blk * BAND, (blk + 1) * BAND)
        return jnp.stack([ref[rows, hh * HEAD_DIM:(hh + 1) * HEAD_DIM] for hh in range(HEADS_PER_GROUP)])

    for blk in range(_ATTN_QBLOCKS):
        q, k_c, v_c = heads(q_ref, blk), heads(kc_ref, blk), heads(vc_ref, blk)
        if blk == 0:
            k_p, v_p = heads(kp_ref, 0), heads(vp_ref, 0)
            mask_prev = mask_band & (i > 0)
        else:
            k_p, v_p = heads(kc_ref, blk - 1), heads(vc_ref, blk - 1)
            mask_prev = mask_band
        s_c = jnp.einsum("hqd,hkd->hqk", q, k_c, preferred_element_type=F32)
        s_p = jnp.einsum("hqd,hkd->hqk", q, k_p, preferred_element_type=F32)
        s_c = jnp.where(mask_cur, s_c, NEG)
        s_p = jnp.where(mask_prev, s_p, NEG)
        mx = jnp.maximum(jnp.max(s_c, axis=-1, keepdims=True), jnp.max(s_p, axis=-1, keepdims=True))
        p_c = jnp.exp(s_c - mx)
        p_p = jnp.exp(s_p - mx)
        den = jnp.sum(p_c, axis=-1, keepdims=True) + jnp.sum(p_p, axis=-1, keepdims=True)
        o = jnp.einsum("hqk,hkd->hqd", p_c.astype(BF16), v_c, preferred_element_type=F32)
        o = o + jnp.einsum("hqk,hkd->hqd", p_p.astype(BF16), v_p, preferred_element_type=F32)
        o = o / den
        lse = mx + jnp.log(den)
        lse_tile = jnp.zeros((BAND, LANES), F32)
        rows = slice(blk * BAND, (blk + 1) * BAND) if dil == 1 else pl.ds(blk * BAND * dil + r, BAND, stride=dil)
        for hh in range(HEADS_PER_GROUP):
            if dil == 1:
                o_ref[rows, hh * HEAD_DIM:(hh + 1) * HEAD_DIM] = o[hh].astype(o_ref.dtype)
            else:
                scratch[0][hh, rows, :] = o[hh]
            lse_tile = jnp.where(col == hh, lse[hh], lse_tile)
        lse_ref[rows, :] = lse_tile

    if dil > 1:
        @pl.when(r == dil - 1)
        def _():
            for hh in range(HEADS_PER_GROUP):
                o_ref[:, hh * HEAD_DIM:(hh + 1) * HEAD_DIM] = scratch[0][hh].astype(o_ref.dtype)


def _band_attention(qkv, g, batch, seq):
    _, dil = ATTN_GROUPS[g]
    sub = seq // dil
    qrows = _ATTN_QBLOCKS * BAND
    nstep = sub // qrows
    rows = qrows * dil

    def cur(col):
        return pl.BlockSpec((None, None, qrows, ATTN_WIDTH), lambda b, i, r: (b, r, i, col))

    def prev(col):
        return pl.BlockSpec((None, None, BAND, ATTN_WIDTH),
                            lambda b, i, r: (b, r, jnp.maximum(i * _ATTN_QBLOCKS - 1, 0), col))

    return pl.pallas_call(
        functools.partial(_band_attn_kernel, dil=dil),
        grid=(batch, nstep, dil),
        in_specs=[cur(0), prev(1), cur(1), prev(2), cur(2)],
        out_specs=[pl.BlockSpec((rows, ATTN_WIDTH), lambda b, i, r: (b * nstep + i, 0)),
                   pl.BlockSpec((rows, LANES), lambda b, i, r: (b * nstep + i, 0))],
        out_shape=[jax.ShapeDtypeStruct((batch * seq, ATTN_WIDTH), BF16),
                   jax.ShapeDtypeStruct((batch * seq, LANES), F32)],
        scratch_shapes=[pltpu.VMEM((HEADS_PER_GROUP, rows, HEAD_DIM), F32)] if dil > 1 else [],
        compiler_params=_params(("parallel", "parallel", "arbitrary")),
        name=f"band_attn_g{g}",
    )(qkv, qkv, qkv, qkv, qkv)


def _combine_kernel(o0_ref, o1_ref, o2_ref, l0_ref, l1_ref, l2_ref, y_ref):
    l0, l1, l2 = l0_ref[...], l1_ref[...], l2_ref[...]
    mx = jnp.maximum(jnp.maximum(l0, l1), l2)
    e0, e1, e2 = jnp.exp(l0 - mx), jnp.exp(l1 - mx), jnp.exp(l2 - mx)
    den = e0 + e1 + e2
    a0, a1 = e0 / den, e1 / den
    rows = l0.shape[0]
    for hh in range(HEADS_PER_GROUP):
        sl = slice(hh * HEAD_DIM, (hh + 1) * HEAD_DIM)
        b0 = jnp.broadcast_to(a0[:, hh:hh + 1], (rows, HEAD_DIM))
        b1 = jnp.broadcast_to(a1[:, hh:hh + 1], (rows, HEAD_DIM))
        o2 = o2_ref[:, sl].astype(F32)
        y = o2 + b0 * (o0_ref[:, sl].astype(F32) - o2) + b1 * (o1_ref[:, sl].astype(F32) - o2)
        y_ref[:, sl] = y.astype(y_ref.dtype)


def _combine(os, lses, tm=512):
    m = os[0].shape[0]
    ospec = pl.BlockSpec((tm, ATTN_WIDTH), lambda i: (i, 0))
    lspec = pl.BlockSpec((tm, LANES), lambda i: (i, 0))
    return pl.pallas_call(
        _combine_kernel,
        grid=(m // tm,),
        in_specs=[ospec, ospec, ospec, lspec, lspec, lspec],
        out_specs=ospec,
        out_shape=jax.ShapeDtypeStruct((m, ATTN_WIDTH), BF16),
        compiler_params=_params(("parallel",)),
        name="attn_combine",
    )(*os, *lses)


_SGU_ROWS = 4 * CHUNK


def _sgu_kernel(u0_ref, u1_ref, v0_ref, v1_ref, g_ref, b_ref, ws_ref, bt_ref, o_ref):
    half = SGU_WIDTH // 2
    nch = v0_ref.shape[0] // CHUNK
    v0 = v0_ref[...].astype(F32)
    v1 = v1_ref[...].astype(F32)
    mu = (jnp.sum(v0, axis=-1, keepdims=True) + jnp.sum(v1, axis=-1, keepdims=True)) / SGU_WIDTH
    d0, d1 = v0 - mu, v1 - mu
    var = (jnp.sum(d0 * d0, axis=-1, keepdims=True) + jnp.sum(d1 * d1, axis=-1, keepdims=True)) / SGU_WIDTH
    inv = lax.rsqrt(var + EPS)
    vn = ((d0 * inv * g_ref[:, :half] + b_ref[:, :half]).astype(BF16),
          (d1 * inv * g_ref[:, half:] + b_ref[:, half:]).astype(BF16))
    u_refs = (u0_ref, u1_ref)
    row = lax.broadcasted_iota(jnp.int32, (CHUNK, CHUNK), 0)
    col = lax.broadcasted_iota(jnp.int32, (CHUNK, CHUNK), 1)
    tri = col <= row
    per_half = SGU_GROUPS // 2
    for gg in range(SGU_GROUPS):
        hf, gl = divmod(gg, per_half)
        sl = slice(gl * SGU_GROUP_CH, (gl + 1) * SGU_GROUP_CH)
        osl = slice(gg * SGU_GROUP_CH, (gg + 1) * SGU_GROUP_CH)
        w = jnp.where(tri, ws_ref[gg], 0.0).astype(BF16)
        v_all = jnp.concatenate([vn[hf][c * CHUNK:(c + 1) * CHUNK, sl] for c in range(nch)], axis=1)
        mixed = jnp.dot(w, v_all, preferred_element_type=F32)
        bias = bt_ref[:, gg:gg + 1]
        for c in range(nch):
            rs = slice(c * CHUNK, (c + 1) * CHUNK)
            m_c = mixed[:, c * SGU_GROUP_CH:(c + 1) * SGU_GROUP_CH] + bias
            o_ref[rs, osl] = (u_refs[hf][rs, sl].astype(F32) * m_c).astype(o_ref.dtype)


def _sgu(proj, ln_g, ln_b, w_spatial, b_spatial):
    m = proj.shape[0]
    half = SGU_WIDTH // 2
    vec = pl.BlockSpec((1, SGU_WIDTH), lambda i: (0, 0))

    def col_spec(c):
        return pl.BlockSpec((_SGU_ROWS, half), lambda i: (i, c))

    return pl.pallas_call(
        _sgu_kernel,
        grid=(m // _SGU_ROWS,),
        in_specs=[col_spec(0), col_spec(1), col_spec(2), col_spec(3),
                  vec, vec,
                  pl.BlockSpec((SGU_GROUPS, CHUNK, CHUNK), lambda i: (0, 0, 0)),
                  pl.BlockSpec((CHUNK, SGU_GROUPS), lambda i: (0, 0))],
        out_specs=pl.BlockSpec((_SGU_ROWS, SGU_WIDTH), lambda i: (i, 0)),
        out_shape=jax.ShapeDtypeStruct((m, SGU_WIDTH), BF16),
        compiler_params=_params(("parallel",)),
        name="sgu",
    )(proj, proj, proj, proj, ln_g.reshape(1, -1), ln_b.reshape(1, -1), w_spatial, b_spatial.T)


def _merge_kernel(h_ref, ya_ref, yb_ref, wga_ref, wgb_ref, wa_ref, wb_ref, bga_ref, bgb_ref, o_ref):
    h = h_ref[...]
    ga = jax.nn.sigmoid(jnp.dot(h, wga_ref[...], preferred_element_type=F32) + bga_ref[...])
    a = jnp.dot(ya_ref[...], wa_ref[...], preferred_element_type=F32)
    acc = ga * a
    gb = jax.nn.sigmoid(jnp.dot(h, wgb_ref[...], preferred_element_type=F32) + bgb_ref[...])
    b = jnp.dot(yb_ref[...], wb_ref[...], preferred_element_type=F32)
    o_ref[...] = (acc + gb * b).astype(o_ref.dtype)


def _merge(h, ya, yb, w_gate, b_gate, w_a, w_b, cast_ws, tm=512, tn=512):
    m = h.shape[0]
    nb = D_MODEL // tn
    b_gate = b_gate.reshape(1, -1)
    return _call_with_casts(
        _merge_kernel, cast_ws,
        grid=(m // tm, nb),
        in_specs=[pl.BlockSpec((tm, D_MODEL), lambda i, j: (i, 0)),
                  pl.BlockSpec((tm, ATTN_WIDTH), lambda i, j: (i, 0)),
                  pl.BlockSpec((tm, SGU_WIDTH), lambda i, j: (i, 0)),
                  pl.BlockSpec((D_MODEL, tn), lambda i, j: (0, j)),
                  pl.BlockSpec((D_MODEL, tn), lambda i, j: (0, j + nb)),
                  pl.BlockSpec((ATTN_WIDTH, tn), lambda i, j: (0, j)),
                  pl.BlockSpec((SGU_WIDTH, tn), lambda i, j: (0, j)),
                  pl.BlockSpec((1, tn), lambda i, j: (0, j)),
                  pl.BlockSpec((1, tn), lambda i, j: (0, j + nb))],
        out_specs=[pl.BlockSpec((tm, tn), lambda i, j: (i, j))],
        out_shape=[jax.ShapeDtypeStruct((m, D_MODEL), BF16)],
        args=(h, ya, yb, w_gate, w_gate, w_a, w_b, b_gate, b_gate),
        compiler_params=_params(("parallel", "arbitrary")),
        name="gated_merge",
    )


def _mlp_up_kernel(a_ref, w_ref, o_ref):
    r = jnp.maximum(jnp.dot(a_ref[...], w_ref[...], preferred_element_type=F32), 0.0)
    o_ref[...] = (r * r).astype(o_ref.dtype)


def _mlp_up(a, w, cast_ws, tm=1024, tn=1024):
    m, k = a.shape
    n = w.shape[1]
    return _call_with_casts(
        _mlp_up_kernel, cast_ws,
        grid=(m // tm, n // tn),
        in_specs=[pl.BlockSpec((tm, k), lambda i, j: (i, 0)), pl.BlockSpec((k, tn), lambda i, j: (0, j))],
        out_specs=[pl.BlockSpec((tm, tn), lambda i, j: (i, j))],
        out_shape=[jax.ShapeDtypeStruct((m, n), BF16)],
        args=(a, w),
        compiler_params=_params(("parallel", "arbitrary")),
        name="mlp_up",
    )


_NORM_DOT_CHUNK = 1024


def _matmul_norm_kernel(a_ref, w_ref, x_ref, g_ref, xo_ref, xs_ref):
    k = pl.program_id(1)
    d = xo_ref.shape[1]
    xw = x_ref.shape[1]
    xs_ref[:, pl.ds(pl.multiple_of(k * xw, xw), xw)] = x_ref[...]
    a = a_ref[...]
    for c in range(d // _NORM_DOT_CHUNK):
        sl = slice(c * _NORM_DOT_CHUNK, (c + 1) * _NORM_DOT_CHUNK)
        part = jnp.dot(a, w_ref[:, sl], preferred_element_type=F32)
        xo_ref[:, sl] = jnp.where(k == 0, 0.0, xo_ref[:, sl]) + part

    @pl.when(k == pl.num_programs(1) - 1)
    def _():
        xo_ref[...] = xs_ref[...] + _rms(xo_ref[...], g_ref[...])


def _matmul_norm_residual(a, w, x, g, name, tm=512, tk=1024):
    m, kdim = a.shape
    d = w.shape[1]
    nk = kdim // tk
    return pl.pallas_call(
        _matmul_norm_kernel,
        grid=(m // tm, nk),
        in_specs=[pl.BlockSpec((tm, tk), lambda i, k: (i, k)),
                  pl.BlockSpec((tk, d), lambda i, k: (k, 0)),
                  pl.BlockSpec((tm, d // nk), lambda i, k: (i, k)),
                  pl.BlockSpec((1, d), lambda i, k: (0, 0))],
        out_specs=pl.BlockSpec((tm, d), lambda i, k: (i, 0)),
        out_shape=jax.ShapeDtypeStruct((m, d), F32),
        scratch_shapes=[pltpu.VMEM((tm, d), F32)],
        compiler_params=_params(("parallel", "arbitrary")),
        name=name,
    )(a, w, x, g.reshape(1, d))


def _resident_norm_kernel(a_ref, w_ref, x_ref, g_ref, xo_ref):
    a = a_ref[...]
    d = xo_ref.shape[1]
    parts = [jnp.dot(a, w_ref[:, c * _NORM_DOT_CHUNK:(c + 1) * _NORM_DOT_CHUNK], preferred_element_type=F32)
             for c in range(d // _NORM_DOT_CHUNK)]
    y = jnp.concatenate(parts, axis=-1)
    xo_ref[...] = x_ref[...] + _rms(y, g_ref[...])


def _resident_norm_residual(a, w, x, g, name, tm=256):
    m, k = a.shape
    d = w.shape[1]
    row = lambda width: pl.BlockSpec((tm, width), lambda i: (i, 0))
    return pl.pallas_call(
        _resident_norm_kernel,
        grid=(m // tm,),
        in_specs=[row(k),
                  pl.BlockSpec((k, d), lambda i: (0, 0), pipeline_mode=pl.Buffered(1)),
                  row(d),
                  pl.BlockSpec((1, d), lambda i: (0, 0))],
        out_specs=row(d),
        out_shape=jax.ShapeDtypeStruct((m, d), F32),
        compiler_params=_params(("arbitrary",), RESIDENT_VMEM_LIMIT),
        name=name,
    )(a, w, x, g.reshape(1, d))


def _mem_kv_kernel(m_ref, g_ref, wk_ref, wv_ref, k_ref, v_ref):
    mn = _rms(m_ref[...], g_ref[...]).astype(BF16)
    k_ref[...] = jnp.dot(mn, wk_ref[...], preferred_element_type=F32).astype(k_ref.dtype)
    v_ref[...] = jnp.dot(mn, wv_ref[...], preferred_element_type=F32).astype(v_ref.dtype)


def _mem_kv(mem2d, g, w_xk, w_xv, tm=256):
    m, d = mem2d.shape
    wspec = pl.BlockSpec((d, XA_WIDTH), lambda i: (0, 0))
    ospec = pl.BlockSpec((tm, XA_WIDTH), lambda i: (i, 0))
    oshape = jax.ShapeDtypeStruct((m, XA_WIDTH), BF16)
    return pl.pallas_call(
        _mem_kv_kernel,
        grid=(m // tm,),
        in_specs=[pl.BlockSpec((tm, d), lambda i: (i, 0)), pl.BlockSpec((1, d), lambda i: (0, 0)), wspec, wspec],
        out_specs=[ospec, ospec],
        out_shape=[oshape, oshape],
        compiler_params=_params(("parallel",)),
        name="mem_kv",
    )(mem2d, g.reshape(1, d), w_xk, w_xv)


def _xattn_kernel(x_ref, gpre_ref, wq_ref, k_ref, v_ref, wo_ref, gpost_ref, gnext_ref, xo_ref, hn_ref):
    x = x_ref[...]
    h = _rms(x, gpre_ref[...]).astype(BF16)
    q = jnp.dot(h, wq_ref[...], preferred_element_type=F32) * ATTN_SCALE
    q = q.astype(BF16)
    dn = (((1,), (1,)), ((), ()))
    outs = []
    for hh in range(XA_HEADS):
        sl = slice(hh * HEAD_DIM, (hh + 1) * HEAD_DIM)
        s = lax.dot_general(q[:, sl], k_ref[:, sl], dn, preferred_element_type=F32)
        mx = jnp.max(s, axis=-1, keepdims=True)
        p = jnp.exp(s - mx)
        den = jnp.sum(p, axis=-1, keepdims=True)
        o = jnp.dot(p.astype(BF16), v_ref[:, sl], preferred_element_type=F32)
        outs.append((o / den).astype(BF16))
    o_all = jnp.concatenate(outs, axis=-1)
    y = jnp.dot(o_all, wo_ref[...], preferred_element_type=F32)
    x_new = x + _rms(y, gpost_ref[...])
    xo_ref[...] = x_new
    hn_ref[...] = _rms(x_new, gnext_ref[...]).astype(hn_ref.dtype)


def _xattn(x, g_pre, w_xq, k_mem, v_mem, w_xo, g_post, g_next, batch, seq, tm=512):
    m, d = x.shape
    n_mem = k_mem.shape[0] // batch
    per_b = seq // tm
    row = pl.BlockSpec((tm, d), lambda i: (i, 0))
    vec = pl.BlockSpec((1, d), lambda i: (0, 0))
    return pl.pallas_call(
        _xattn_kernel,
        grid=(m // tm,),
        in_specs=[row, vec,
                  pl.BlockSpec((d, XA_WIDTH), lambda i: (0, 0), pipeline_mode=pl.Buffered(1)),
                  pl.BlockSpec((n_mem, XA_WIDTH), lambda i: (i // per_b, 0)),
                  pl.BlockSpec((n_mem, XA_WIDTH), lambda i: (i // per_b, 0)),
                  pl.BlockSpec((XA_WIDTH, d), lambda i: (0, 0), pipeline_mode=pl.Buffered(1)),
                  vec, vec],
        out_specs=[row, row],
        out_shape=[jax.ShapeDtypeStruct((m, d), F32), jax.ShapeDtypeStruct((m, d), BF16)],
        compiler_params=_params(("parallel",), RESIDENT_VMEM_LIMIT),
        name="cross_attn",
    )(x, g_pre.reshape(1, d), w_xq, k_mem, v_mem, w_xo, g_post.reshape(1, d), g_next.reshape(1, d))


def kernel(x, mem, positions, mix_pre_g, w_in, sgu_ln_g, sgu_ln_b, w_spatial, b_spatial, w_branch_a, w_branch_b, w_gate, b_gate, w_out, mix_post_g, xa_pre_g, mem_norm_g, w_xq, w_xk, w_xv, w_xo, xa_post_g, mlp_pre_g, w_up, w_down, mlp_post_g):
    batch, seq, d = x.shape
    n_mem = mem.shape[1]
    depth = w_in.shape[0]
    m = batch * seq
    x2 = x.reshape(m, d)
    mem2 = mem.reshape(batch * n_mem, d)

    for l in range(depth):
        w_in_bf = w_in[l].astype(BF16)

        if l == 0:
            h, c_tab, sa_tab, sb_tab = _rmsnorm(x2, mix_pre_g[l], positions)
        else:
            h = _rmsnorm(x2, mix_pre_g[l])
        uv, w_gate_bf, w_a_bf, w_b_bf = _inproj_uv(h, w_in_bf, (w_gate[l], w_branch_a[l], w_branch_b[l]))
        os, lses = [], []
        for g in range(N_GROUPS):
            qkv = _inproj_qkv(h, w_in_bf, c_tab, sa_tab, sb_tab, g, batch, seq)
            o, lse = _band_attention(qkv, g, batch, seq)
            os.append(o)
            lses.append(lse)
        y_a = _combine(os, lses)
        y_b = _sgu(uv, sgu_ln_g[l], sgu_ln_b[l], w_spatial[l], b_spatial[l])
        merged, w_up_bf, w_out_bf, w_xq_bf, w_xk_bf, w_xv_bf, w_xo_bf = _merge(
            h, y_a, y_b, w_gate_bf, b_gate[l], w_a_bf, w_b_bf,
            (w_up[l], w_out[l], w_xq[l], w_xk[l], w_xv[l], w_xo[l]))
        x2 = _resident_norm_residual(merged, w_out_bf, x2, mix_post_g[l], "out_proj_norm")

        k_mem, v_mem = _mem_kv(mem2, mem_norm_g[l], w_xk_bf, w_xv_bf)
        x2, h = _xattn(x2, xa_pre_g[l], w_xq_bf, k_mem, v_mem, w_xo_bf, xa_post_g[l], mlp_pre_g[l], batch, seq)

        a, w_down_bf = _mlp_up(h, w_up_bf, (w_down[l],))
        x2 = _matmul_norm_residual(a, w_down_bf, x2, mlp_post_g[l], "mlp_down_norm")
    return x2.reshape(batch, seq, d)
```
